```python
import math
import jax
import jax.numpy as jnp
from jax import lax
import numpy as np

D_MODEL = 1024
BATCH = 16
SEQ = 4096
DEPTH = 1
DEC_BATCH = 8
DEC_SEQ = 16
PAST_LEN = 4096

CHUNK = 64
A_PREV_CHUNKS = 8
A_PAST = A_PREV_CHUNKS * CHUNK
A_HEADS = 8
A_HEAD_DIM = 64
REL_CLIP = 128
B_HEADS = 4
B_HEAD_DIM = 64
B_V_DIM = 2 * B_HEAD_DIM
M_HEADS = 4
M_HEAD_DIM = 128
N_MEM = 256
BRANCH_W = 512
N_BRANCH = 3
IN_SPLITS = (BRANCH_W,) * 10 + (N_BRANCH * D_MODEL,)
D_IN = 10 * BRANCH_W + N_BRANCH * D_MODEL
ROPE_THETA = 10000.0
RMS_EPS = 1e-6
Q_BLOCK = 128
NEG_INF = -1e30

kernel_name = 'hybrid_chunk_diff_mem_encoder_step'


def rmsnorm(x, g):
    xf = x.astype(jnp.float32)
    xf = xf * lax.rsqrt(jnp.mean(xf * xf, axis=-1, keepdims=True) + RMS_EPS)
    return (xf * g.astype(jnp.float32)).astype(x.dtype)


def rope(x, pos):
    d = x.shape[-1]
    inv = 1.0 / (ROPE_THETA ** (jnp.arange(0, d, 2, dtype=jnp.float32) / d))
    ang = pos.astype(jnp.float32)[:, None] * inv[None, :]
    cos = jnp.cos(ang)[:, None, :]
    sin = jnp.sin(ang)[:, None, :]
    xf = x.astype(jnp.float32)
    x1, x2 = xf[..., : d // 2], xf[..., d // 2:]
    return jnp.concatenate([x1 * cos - x2 * sin, x1 * sin + x2 * cos], axis=-1).astype(x.dtype)


def lambda_init(layer):
    return 0.8 - 0.6 * math.exp(-0.3 * layer)


def diff_lambda(lq1, lk1, lq2, lk2, lam_init):
    e = lambda a, b: jnp.exp(jnp.sum(a.astype(jnp.float32) * b.astype(jnp.float32)))
    return e(lq1, lk1) - e(lq2, lk2) + lam_init


def rel_bias_lookup(rel_bias, dist):
    idx = jnp.clip(dist, -REL_CLIP, REL_CLIP) + REL_CLIP
    return rel_bias[:, idx].astype(jnp.float32)


def project_in(x, pos, norm_in, w_in):
    b, s, _ = x.shape
    h = rmsnorm(x, norm_in)
    proj = jnp.einsum('bsd,de->bse', h, w_in)
    bounds = np.cumsum(IN_SPLITS)[:-1].tolist()
    aq, ak, av, az, bq, bk, bv, bz, mq, mz, g = jnp.split(proj, bounds, axis=-1)
    r = lambda a, nh, d: a.reshape(b, s, nh, d)
    return (r(aq, A_HEADS, A_HEAD_DIM), r(ak, A_HEADS, A_HEAD_DIM), r(av, A_HEADS, A_HEAD_DIM),
            rope(r(bq, 2 * B_HEADS, B_HEAD_DIM), pos), rope(r(bk, 2 * B_HEADS, B_HEAD_DIM), pos),
            r(bv, B_HEADS, B_V_DIM), r(mq, M_HEADS, M_HEAD_DIM), az, bz, mz, g)


def chunk_attn_prompt(q, k, v, rel_bias):
    b, s, h, d = q.shape
    n_chunks = s // CHUNK
    band = A_PAST + CHUNK
    kp = jnp.pad(k, ((0, 0), (A_PAST, 0), (0, 0), (0, 0)))
    vp = jnp.pad(v, ((0, 0), (A_PAST, 0), (0, 0), (0, 0)))
    qc = jnp.moveaxis(q.reshape(b, n_chunks, CHUNK, h, d), 1, 0)
    dist = jnp.arange(CHUNK)[:, None] + A_PAST - jnp.arange(band)[None, :]
    bias = rel_bias_lookup(rel_bias, dist)
    scale = d ** -0.5

    def one_chunk(args):
        c, qb = args
        start = c * CHUNK
        kb = lax.dynamic_slice_in_dim(kp, start, band, axis=1)
        vb = lax.dynamic_slice_in_dim(vp, start, band, axis=1)
        valid = (start - A_PAST + jnp.arange(band)) >= 0
        sc = jnp.einsum('bqhd,bkhd->bhqk', qb, kb).astype(jnp.float32) * scale + bias
        sc = jnp.where(valid, sc, NEG_INF)
        p = jax.nn.softmax(sc, axis=-1)
        return jnp.einsum('bhqk,bkhd->bqhd', p.astype(vb.dtype), vb)

    o = lax.map(one_chunk, (jnp.arange(n_chunks), qc))
    return jnp.moveaxis(o, 0, 1).reshape(b, s, h * d)


def chunk_attn_sample(q, k_new, v_new, cache_k, cache_v, rel_bias):
    b, t, h, d = q.shape
    p_len = cache_k.shape[1]
    k = jnp.concatenate([cache_k, k_new], axis=1)
    v = jnp.concatenate([cache_v, v_new], axis=1)
    kpos = jnp.concatenate([jnp.arange(p_len), p_len + jnp.arange(t)])
    dist = (p_len + jnp.arange(t))[:, None] - kpos[None, :]
    bias = rel_bias_lookup(rel_bias, dist)
    sc = jnp.einsum('bqhd,bkhd->bhqk', q, k).astype(jnp.float32) * (d ** -0.5) + bias
    p = jax.nn.softmax(sc, axis=-1)
    return jnp.einsum('bhqk,bkhd->bqhd', p.astype(v.dtype), v).reshape(b, t, h * d)


def diff_core(q, k, v, mask, lam, subln, lam_init):
    sc = jnp.einsum('bqhmd,bkhmd->bhmqk', q, k).astype(jnp.float32) * (B_HEAD_DIM ** -0.5)
    if mask is not None:
        sc = jnp.where(mask, sc, NEG_INF)
    p = jax.nn.softmax(sc, axis=-1)
    pd = p[:, :, 0] - lam * p[:, :, 1]
    o = jnp.einsum('bhqk,bkhe->bqhe', pd.astype(v.dtype), v)
    return rmsnorm(o, subln) * (1.0 - lam_init)


def diff_attn_prompt(q, k, v, lam, subln, lam_init):
    b, s = q.shape[:2]
    nq = s // Q_BLOCK
    qb = jnp.moveaxis(q.reshape(b, nq, Q_BLOCK, B_HEADS, 2, B_HEAD_DIM), 1, 0)
    k2 = k.reshape(b, s, B_HEADS, 2, B_HEAD_DIM)
    k_chunk = jnp.arange(s) // CHUNK

    def one_block(args):
        i, qblk = args
        q_chunk = (i * Q_BLOCK + jnp.arange(Q_BLOCK)) // CHUNK
        mask = k_chunk[None, :] <= q_chunk[:, None]
        return diff_core(qblk, k2, v, mask, lam, subln, lam_init)

    o = lax.map(one_block, (jnp.arange(nq), qb))
    return jnp.moveaxis(o, 0, 1).reshape(b, s, B_HEADS * B_V_DIM)


def diff_attn_sample(q, k_new, v_new, cache_k, cache_v, lam, subln, lam_init):
    b, t = q.shape[:2]
    k = jnp.concatenate([cache_k, k_new], axis=1)
    v = jnp.concatenate([cache_v, v_new], axis=1)
    n_k = k.shape[1]
    o = diff_core(q.reshape(b, t, B_HEADS, 2, B_HEAD_DIM), k.reshape(b, n_k, B_HEADS, 2, B_HEAD_DIM),
                  v, None, lam, subln, lam_init)
    return o.reshape(b, t, B_HEADS * B_V_DIM)


def memory_kv(mem, norm_mem, w_mem_kv):
    b, n, _ = mem.shape
    kv = jnp.einsum('bnd,de->bne', rmsnorm(mem, norm_mem), w_mem_kv)
    mk, mv = jnp.split(kv, 2, axis=-1)
    return mk.reshape(b, n, M_HEADS, M_HEAD_DIM), mv.reshape(b, n, M_HEADS, M_HEAD_DIM)


def mem_attn(q, mk, mv):
    b, s = q.shape[:2]
    sc = jnp.einsum('bqhd,bmhd->bhqm', q, mk).astype(jnp.float32) * (M_HEAD_DIM ** -0.5)
    p = jax.nn.softmax(sc, axis=-1)
    return jnp.einsum('bhqm,bmhd->bqhd', p.astype(mv.dtype), mv).reshape(b, s, M_HEADS * M_HEAD_DIM)


def merge_branches(o_a, z_a, o_b, z_b, o_m, z_m, g, w_ba, w_bb, w_bm, w_out):
    p_a = jnp.einsum('bse,ed->bsd', o_a * jax.nn.silu(z_a), w_ba)
    p_b = jnp.einsum('bse,ed->bsd', o_b * jax.nn.silu(z_b), w_bb)
    p_m = jnp.einsum('bse,ed->bsd', o_m * jax.nn.silu(z_m), w_bm)
    g_a, g_b, g_m = jnp.split(jax.nn.sigmoid(g), N_BRANCH, axis=-1)
    mixed = g_a * p_a + g_b * p_b + g_m * p_m
    return jnp.einsum('bsd,de->bse', mixed, w_out)


def setup_inputs(seed: int = 0) -> dict:
    key = jax.random.key(seed)
    ks = jax.random.split(key, 24)
    nrm = lambda k, shape, scale: jax.random.normal(k, shape, jnp.float32) * scale
    a_cache = min(A_PAST, PAST_LEN)
    return {
        'x_prompt': nrm(ks[0], (BATCH, SEQ, D_MODEL), 1.0),
        'x_sample': nrm(ks[1], (DEC_BATCH, DEC_SEQ, D_MODEL), 1.0),
        'cache_a_k': nrm(ks[2], (DEPTH, DEC_BATCH, a_cache, A_HEADS, A_HEAD_DIM), 1.0),
        'cache_a_v': nrm(ks[3], (DEPTH, DEC_BATCH, a_cache, A_HEADS, A_HEAD_DIM), 1.0),
        'cache_b_k': nrm(ks[4], (DEPTH, DEC_BATCH, PAST_LEN, 2 * B_HEADS, B_HEAD_DIM), 1.0),
        'cache_b_v': nrm(ks[5], (DEPTH, DEC_BATCH, PAST_LEN, B_HEADS, B_V_DIM), 1.0),
        'cache_mem_k': nrm(ks[6], (DEPTH, DEC_BATCH, N_MEM, M_HEADS, M_HEAD_DIM), 1.0),
        'cache_mem_v': nrm(ks[7], (DEPTH, DEC_BATCH, N_MEM, M_HEADS, M_HEAD_DIM), 1.0),
        'mem_prompt': nrm(ks[8], (BATCH, N_MEM, D_MODEL), 1.0),
        'norm_in': 1.0 + nrm(ks[9], (DEPTH, D_MODEL), 0.02),
        'w_in': nrm(ks[10], (DEPTH, D_MODEL, D_IN), D_MODEL ** -0.5),
        'rel_bias': nrm(ks[11], (DEPTH, A_HEADS, 2 * REL_CLIP + 1), 0.5),
        'lambda_q1': nrm(ks[12], (DEPTH, B_HEAD_DIM), 0.1),
        'lambda_k1': nrm(ks[13], (DEPTH, B_HEAD_DIM), 0.1),
        'lambda_q2': nrm(ks[14], (DEPTH, B_HEAD_DIM), 0.1),
        'lambda_k2': nrm(ks[15], (DEPTH, B_HEAD_DIM), 0.1),
        'subln': 1.0 + nrm(ks[16], (DEPTH, B_V_DIM), 0.02),
        'norm_mem': 1.0 + nrm(ks[17], (DEPTH, D_MODEL), 0.02),
        'w_mem_kv': nrm(ks[18], (DEPTH, D_MODEL, 2 * M_HEADS * M_HEAD_DIM), D_MODEL ** -0.5),
        'w_branch_a': nrm(ks[19], (DEPTH, BRANCH_W, D_MODEL), BRANCH_W ** -0.5),
        'w_branch_b': nrm(ks[20], (DEPTH, BRANCH_W, D_MODEL), BRANCH_W ** -0.5),
        'w_branch_m': nrm(ks[21], (DEPTH, BRANCH_W, D_MODEL), BRANCH_W ** -0.5),
        'w_out': nrm(ks[22], (DEPTH, D_MODEL, D_MODEL), D_MODEL ** -0.5),
        'norm_final': 1.0 + nrm(ks[23], (D_MODEL,), 0.02),
    }


def reference(x_prompt, x_sample, cache_a_k, cache_a_v, cache_b_k, cache_b_v, cache_mem_k, cache_mem_v,
              mem_prompt, norm_in, w_in, rel_bias, lambda_q1, lambda_k1, lambda_q2, lambda_k2, subln,
              norm_mem, w_mem_kv, w_branch_a, w_branch_b, w_branch_m, w_out, norm_final):
    hp, hs = x_prompt, x_sample
    s = x_prompt.shape[1]
    t = x_sample.shape[1]
    past = cache_b_k.shape[2]
    pos_p = jnp.arange(s)
    pos_s = past + jnp.arange(t)
    keep = min(A_PAST, s)
    new = [[] for _ in range(10)]
    for l in range(DEPTH):
        lam_init = lambda_init(l)
        lam = diff_lambda(lambda_q1[l], lambda_k1[l], lambda_q2[l], lambda_k2[l], lam_init)
        qa, ka, va, qb, kb, vb, qm, za, zb, zm, g = project_in(hp, pos_p, norm_in[l], w_in[l])
        o_a = chunk_attn_prompt(qa, ka, va, rel_bias[l])
        o_b = diff_attn_prompt(qb, kb, vb, lam, subln[l], lam_init)
        mk, mv = memory_kv(mem_prompt, norm_mem[l], w_mem_kv[l])
        o_m = mem_attn(qm, mk, mv)
        hp = hp + merge_branches(o_a, za, o_b, zb, o_m, zm, g, w_branch_a[l], w_branch_b[l], w_branch_m[l], w_out[l])
        new[0].append(ka[:, s - keep:])
        new[1].append(va[:, s - keep:])
        new[2].append(kb)
        new[3].append(vb)
        new[4].append(mk)
        new[5].append(mv)
        qa, ka, va, qb, kb, vb, qm, za, zb, zm, g = project_in(hs, pos_s, norm_in[l], w_in[l])
        o_a = chunk_attn_sample(qa, ka, va, cache_a_k[l], cache_a_v[l], rel_bias[l])
        o_b = diff_attn_sample(qb, kb, vb, cache_b_k[l], cache_b_v[l], lam, subln[l], lam_init)
        o_m = mem_attn(qm, cache_mem_k[l], cache_mem_v[l])
        hs = hs + merge_branches(o_a, za, o_b, zb, o_m, zm, g, w_branch_a[l], w_branch_b[l], w_branch_m[l], w_out[l])
        new[6].append(ka)
        new[7].append(va)
        new[8].append(kb)
        new[9].append(vb)
    y_prompt = rmsnorm(hp, norm_final)
    y_sample = rmsnorm(hs, norm_final)
    return (y_prompt, y_sample, jnp.stack(new[0]), jnp.stack(new[1]), jnp.stack(new[2]), jnp.stack(new[3]),
            jnp.stack(new[4]), jnp.stack(new[5]), jnp.stack(new[6]), jnp.stack(new[7]), jnp.stack(new[8]),
            jnp.stack(new[9]))
```

```python
import functools
import math

import jax
import jax.numpy as jnp
from jax import lax
from jax.experimental import pallas as pl
from jax.experimental.pallas import tpu as pltpu

F32 = jnp.float32
BF16 = jnp.bfloat16

CHUNK = 64
CHUNK_SHIFT = 6
A_PAST = 8 * CHUNK
REL_CLIP = 128
BRANCH_W = 512
SLAB = 128
HALF = 64
ROPE_THETA = 10000.0
RMS_EPS = 1e-6
NEG_INF = -1e30

PROJ_ROWS = 512
VT_BLK = 256
KV_STEP = 512
Q_BLK = 128
BAND = A_PAST + Q_BLK

VMEM_LIMIT = 56 * 1024 * 1024


def _dot(a, b):
    return jnp.dot(a, b, preferred_element_type=F32)


def _dot_nt(a, b):
    return lax.dot_general(a, b, (((1,), (1,)), ((), ())), preferred_element_type=F32)


def _silu(z):
    return z * (1.0 / (1.0 + jnp.exp(-z)))


def _sigmoid(z):
    return 1.0 / (1.0 + jnp.exp(-z))


def _split_pair(q):
    lane = lax.broadcasted_iota(jnp.int32, q.shape, 1)
    zero = jnp.zeros_like(q)
    return jnp.concatenate([jnp.where(lane < HALF, q, zero), jnp.where(lane >= HALF, q, zero)], axis=0)


def _diff_lambda(lq1_ref, lk1_ref, lq2_ref, lk2_ref, lam_init):
    e1 = jnp.exp(jnp.sum(lq1_ref[...] * lk1_ref[...], axis=-1, keepdims=True))
    e2 = jnp.exp(jnp.sum(lq2_ref[...] * lk2_ref[...], axis=-1, keepdims=True))
    return e1 - e2 + lam_init


def _in_proj_kernel(x_ref, g_ref, w_ref, cos_ref, sin_ref,
                    qa_ref, ka_ref, va_ref, kaw_ref, vaw_ref,
                    qb_ref, kb_ref, kbf_ref, vb_ref, vbf_ref,
                    qm_ref, sza_ref, szb_ref, szm_ref, sg_ref, *, transposed_v, mem_scale):
    tm = x_ref.shape[0]
    x = x_ref[...]
    h = (x * lax.rsqrt(jnp.mean(x * x, axis=-1, keepdims=True) + RMS_EPS) * g_ref[...]).astype(BF16)

    def proj(idx):
        return _dot(h, w_ref[:, idx * BRANCH_W:(idx + 1) * BRANCH_W])

    cos = cos_ref[...]
    sin = sin_ref[...]
    lane = lax.broadcasted_iota(jnp.int32, (tm, SLAB), 1)
    first_half = (lane & (HALF // 2)) == 0

    def rope_slabs(acc):
        out = []
        for j in range(BRANCH_W // SLAB):
            y = acc[:, j * SLAB:(j + 1) * SLAB]
            partner = jnp.where(first_half, pltpu.roll(y, SLAB - HALF // 2, 1), pltpu.roll(y, HALF // 2, 1))
            out.append(y * cos + partner * sin)
        return out

    qa_ref[...] = (proj(0) * (HALF ** -0.5)).astype(BF16)
    ak = proj(1)
    ka_ref[...] = ak.astype(BF16)
    kaw_ref[...] = ak
    av = proj(2)
    vaw_ref[...] = av
    if transposed_v:
        for p in range(BRANCH_W // SLAB):
            for i in range(tm // SLAB):
                va_ref[p, i] = av[i * SLAB:(i + 1) * SLAB, p * SLAB:(p + 1) * SLAB].T.astype(BF16)
    else:
        va_ref[...] = av.astype(BF16)
    sza_ref[...] = _silu(proj(3)).astype(BF16)

    for j, y in enumerate(rope_slabs(proj(4))):
        qb_ref[:, j * SLAB:(j + 1) * SLAB] = (y * (HALF ** -0.5)).astype(BF16)
    for j, y in enumerate(rope_slabs(proj(5))):
        kbf_ref[:, j * SLAB:(j + 1) * SLAB] = y
        kb_ref[:, j * SLAB:(j + 1) * SLAB] = y.astype(BF16)
    bv = proj(6)
    vbf_ref[...] = bv
    if transposed_v:
        for hd in range(BRANCH_W // SLAB):
            for i in range(tm // VT_BLK):
                vb_ref[hd, i] = bv[i * VT_BLK:(i + 1) * VT_BLK, hd * SLAB:(hd + 1) * SLAB].T.astype(BF16)
    else:
        vb_ref[...] = bv.astype(BF16)
    szb_ref[...] = _silu(proj(7)).astype(BF16)

    qm_ref[...] = (proj(8) * mem_scale).astype(BF16)
    szm_ref[...] = _silu(proj(9)).astype(BF16)
    for j in range(sg_ref.shape[1] // BRANCH_W):
        sg_ref[:, j * BRANCH_W:(j + 1) * BRANCH_W] = _sigmoid(proj(10 + j)).astype(BF16)


def _in_proj(x, norm_g, w_bf16, cos_tab, sin_tab, *, keep, transposed_v):
    b, s, d = x.shape
    d_in = w_bf16.shape[1]
    tm = min(PROJ_ROWS, s)
    nt = s // tm
    first_kept = (s - keep) // tm
    n_gate = d_in - 10 * BRANCH_W

    row = lambda bi, j: (bi, j, 0)
    kept = lambda bi, j: (bi, jnp.maximum(j - first_kept, 0), 0)
    act = lambda dt: jax.ShapeDtypeStruct((b, s, BRANCH_W), dt)
    act_spec = pl.BlockSpec((None, tm, BRANCH_W), row)
    win = jax.ShapeDtypeStruct((b, keep, BRANCH_W), F32)
    win_spec = pl.BlockSpec((None, tm, BRANCH_W), kept)
    if transposed_v:
        va = jax.ShapeDtypeStruct((b, BRANCH_W // SLAB, s // SLAB, SLAB, SLAB), BF16)
        va_spec = pl.BlockSpec((None, BRANCH_W // SLAB, tm // SLAB, SLAB, SLAB), lambda bi, j: (bi, 0, j, 0, 0))
        vb = jax.ShapeDtypeStruct((b, BRANCH_W // SLAB, s // VT_BLK, SLAB, VT_BLK), BF16)
        vb_spec = pl.BlockSpec((None, BRANCH_W // SLAB, tm // VT_BLK, SLAB, VT_BLK), lambda bi, j: (bi, 0, j, 0, 0))
    else:
        va, va_spec, vb, vb_spec = act(BF16), act_spec, act(BF16), act_spec

    out_shape = [act(BF16), act(BF16), va, win, win,
                 act(BF16), act(BF16), act(F32), vb, act(F32),
                 act(BF16), act(BF16), act(BF16), act(BF16),
                 jax.ShapeDtypeStruct((b, s, n_gate), BF16)]
    out_specs = [act_spec, act_spec, va_spec, win_spec, win_spec,
                 act_spec, act_spec, act_spec, vb_spec, act_spec,
                 act_spec, act_spec, act_spec, act_spec,
                 pl.BlockSpec((None, tm, n_gate), row)]
    return pl.pallas_call(
        functools.partial(_in_proj_kernel, transposed_v=transposed_v, mem_scale=SLAB ** -0.5),
        grid=(b, nt),
        in_specs=[pl.BlockSpec((None, tm, d), row),
                  pl.BlockSpec((1, d), lambda bi, j: (0, 0)),
                  pl.BlockSpec((d, d_in), lambda bi, j: (0, 0), pipeline_mode=pl.Buffered(1)),
                  pl.BlockSpec((tm, SLAB), lambda bi, j: (j, 0)),
                  pl.BlockSpec((tm, SLAB), lambda bi, j: (j, 0))],
        out_specs=out_specs,
        out_shape=out_shape,
        compiler_params=pltpu.CompilerParams(dimension_semantics=("arbitrary", "arbitrary"),
                                             vmem_limit_bytes=VMEM_LIMIT),
        name="in_proj",
    )(x, norm_g, w_bf16, cos_tab, sin_tab)


def _chunk_attn_prompt_kernel(q_ref, k_ref, vt_ref, bias_ref, sz_ref, o_ref):
    s_len = q_ref.shape[0]

    def q_block(g, carry):
        q0 = pl.multiple_of(g * Q_BLK, Q_BLK)
        q2 = _split_pair(q_ref[pl.ds(q0, Q_BLK), :])
        start = g * Q_BLK - A_PAST
        start_c = jnp.maximum(start, 0)
        off = pl.multiple_of(start_c - start, SLAB)
        start_c = pl.multiple_of(start_c, SLAB)
        kb = k_ref[pl.ds(start_c, BAND), :]
        s = _dot_nt(kb, q2) + bias_ref[pl.ds(off, BAND), :]
        m = jnp.max(s, axis=0, keepdims=True)
        p = jnp.exp(s - m)
        l = jnp.sum(p, axis=0, keepdims=True)
        pb = p.astype(BF16)
        blk0 = start_c // SLAB
        acc = jnp.zeros((SLAB, 2 * Q_BLK), F32)
        for i in range(BAND // SLAB):
            acc = acc + _dot(vt_ref[blk0 + i], pb[i * SLAB:(i + 1) * SLAB, :])
        acc = acc * (1.0 / l)
        o_t = jnp.concatenate([acc[:HALF, :Q_BLK], acc[HALF:, Q_BLK:]], axis=0)
        o = o_t.T * sz_ref[pl.ds(q0, Q_BLK), :].astype(F32)
        o_ref[pl.ds(q0, Q_BLK), :] = o.astype(BF16)
        return carry

    lax.fori_loop(0, s_len // Q_BLK, q_block, 0)


def _chunk_attn_prompt(qa, ka, va_t, bias_t, sza):
    b, s, _ = qa.shape
    n_pair = BRANCH_W // SLAB
    slab = pl.BlockSpec((None, s, SLAB), lambda bi, p: (bi, 0, p))
    return pl.pallas_call(
        _chunk_attn_prompt_kernel,
        grid=(b, n_pair),
        in_specs=[slab, slab,
                  pl.BlockSpec((None, None, s // SLAB, SLAB, SLAB), lambda bi, p: (bi, p, 0, 0, 0)),
                  pl.BlockSpec((None, BAND + A_PAST, 2 * Q_BLK), lambda bi, p: (p, 0, 0)),
                  slab],
        out_specs=slab,
        out_shape=jax.ShapeDtypeStruct((b, s, BRANCH_W), BF16),
        compiler_params=pltpu.CompilerParams(dimension_semantics=("arbitrary", "arbitrary"),
                                             vmem_limit_bytes=VMEM_LIMIT),
        name="chunk_attn_prompt",
    )(qa, ka, va_t, bias_t, sza)


def _diff_attn_prompt_kernel(q_ref, k_ref, vt_ref, sz_ref, lq1_ref, lk1_ref, lq2_ref, lk2_ref, subln_ref,
                             o_ref, *, lam_init):
    s_len = q_ref.shape[0]
    lam = _diff_lambda(lq1_ref, lk1_ref, lq2_ref, lk2_ref, lam_init)
    sub_blocks = KV_STEP // VT_BLK

    def q_block(qi, carry):
        q0 = pl.multiple_of(qi * Q_BLK, Q_BLK)
        q2 = _split_pair(q_ref[pl.ds(q0, Q_BLK), :])

        def kv_step(kv, state, masked):
            m, l, acc = state
            k0 = pl.multiple_of(kv * KV_STEP, KV_STEP)
            s = _dot_nt(k_ref[pl.ds(k0, KV_STEP), :], q2)
            if masked:
                k_chunk = jnp.right_shift(k0 + lax.broadcasted_iota(jnp.int32, s.shape, 0), CHUNK_SHIFT)
                q_chunk = jnp.right_shift(q0 + (lax.broadcasted_iota(jnp.int32, s.shape, 1) & (Q_BLK - 1)), CHUNK_SHIFT)
                s = jnp.where(k_chunk <= q_chunk, s, NEG_INF)
            m_new = jnp.maximum(m, jnp.max(s, axis=0, keepdims=True))
            alpha = jnp.exp(m - m_new)
            p = jnp.exp(s - m_new)
            l = alpha * l + jnp.sum(p, axis=0, keepdims=True)
            pb = p.astype(BF16)
            pv = _dot(vt_ref[sub_blocks * kv], pb[:VT_BLK, :])
            for i in range(1, sub_blocks):
                pv = pv + _dot(vt_ref[sub_blocks * kv + i], pb[i * VT_BLK:(i + 1) * VT_BLK, :])
            return m_new, l, alpha * acc + pv

        n_full = (qi * Q_BLK) // KV_STEP
        state = (jnp.full((1, 2 * Q_BLK), NEG_INF, F32), jnp.zeros((1, 2 * Q_BLK), F32),
                 jnp.zeros((SLAB, 2 * Q_BLK), F32))
        state = lax.fori_loop(0, n_full, lambda kv, st: kv_step(kv, st, False), state)
        _, l, acc = kv_step(n_full, state, True)

        acc = acc * (1.0 / l)
        o_t = acc[:, :Q_BLK] - lam * acc[:, Q_BLK:]
        o_t = o_t * lax.rsqrt(jnp.mean(o_t * o_t, axis=0, keepdims=True) + RMS_EPS)
        o_t = o_t * subln_ref[...] * (1.0 - lam_init)
        o = o_t.T * sz_ref[pl.ds(q0, Q_BLK), :].astype(F32)
        o_ref[pl.ds(q0, Q_BLK), :] = o.astype(BF16)
        return carry

    lax.fori_loop(0, s_len // Q_BLK, q_block, 0)


def _diff_attn_prompt(qb, kb, vb_t, szb, lq1, lk1, lq2, lk2, subln_col, lam_init):
    b, s, _ = qb.shape
    n_head = BRANCH_W // SLAB
    slab = pl.BlockSpec((None, s, SLAB), lambda bi, hd: (bi, 0, hd))
    small = lambda shape: pl.BlockSpec(shape, lambda bi, hd: (0, 0))
    return pl.pallas_call(
        functools.partial(_diff_attn_prompt_kernel, lam_init=lam_init),
        grid=(b, n_head),
        in_specs=[slab, slab,
                  pl.BlockSpec((None, None, s // VT_BLK, SLAB, VT_BLK), lambda bi, hd: (bi, hd, 0, 0, 0)),
                  slab,
                  small(lq1.shape), small(lk1.shape), small(lq2.shape), small(lk2.shape), small(subln_col.shape)],
        out_specs=slab,
        out_shape=jax.ShapeDtypeStruct((b, s, BRANCH_W), BF16),
        compiler_params=pltpu.CompilerParams(dimension_semantics=("arbitrary", "arbitrary"),
                                             vmem_limit_bytes=VMEM_LIMIT),
        name="diff_attn_prompt",
    )(qb, kb, vb_t, szb, lq1, lk1, lq2, lk2, subln_col)


def _mem_kv_kernel(mem_ref, g_ref, w_ref, kf_ref, vf_ref, kb_ref, vb_ref):
    x = mem_ref[...]
    h = (x * lax.rsqrt(jnp.mean(x * x, axis=-1, keepdims=True) + RMS_EPS) * g_ref[...]).astype(BF16)
    k = _dot(h, w_ref[:, :BRANCH_W])
    v = _dot(h, w_ref[:, BRANCH_W:])
    kf_ref[...] = k
    vf_ref[...] = v
    kb_ref[...] = k.astype(BF16)
    vb_ref[...] = v.astype(BF16)


def _mem_kv(mem, norm_g, w_bf16):
    b, n, d = mem.shape
    blk = lambda width: pl.BlockSpec((None, n, width), lambda bi: (bi, 0, 0))
    out = lambda dt: jax.ShapeDtypeStruct((b, n, BRANCH_W), dt)
    return pl.pallas_call(
        _mem_kv_kernel,
        grid=(b,),
        in_specs=[blk(d), pl.BlockSpec((1, d), lambda bi: (0, 0)),
                  pl.BlockSpec((d, 2 * BRANCH_W), lambda bi: (0, 0))],
        out_specs=[blk(BRANCH_W)] * 4,
        out_shape=[out(F32), out(F32), out(BF16), out(BF16)],
        compiler_params=pltpu.CompilerParams(dimension_semantics=("arbitrary",)),
        name="mem_kv",
    )(mem, norm_g, w_bf16)


def _mem_attn_kernel(q_ref, k_ref, v_ref, sz_ref, o_ref):
    for hd in range(BRANCH_W // SLAB):
        cols = slice(hd * SLAB, (hd + 1) * SLAB)
        s = _dot_nt(q_ref[:, cols], k_ref[:, cols].astype(BF16))
        m = jnp.max(s, axis=-1, keepdims=True)
        p = jnp.exp(s - m)
        l = jnp.sum(p, axis=-1, keepdims=True)
        o = _dot(p.astype(BF16), v_ref[:, cols].astype(BF16)) * (1.0 / l)
        o_ref[:, cols] = (o * sz_ref[:, cols].astype(F32)).astype(BF16)


def _mem_attn(qm, mk, mv, szm):
    b, s, _ = qm.shape
    n = mk.shape[1]
    tq = min(PROJ_ROWS, s)
    row = pl.BlockSpec((None, tq, BRANCH_W), lambda bi, j: (bi, j, 0))
    mem = pl.BlockSpec((None, n, BRANCH_W), lambda bi, j: (bi, 0, 0))
    return pl.pallas_call(
        _mem_attn_kernel,
        grid=(b, s // tq),
        in_specs=[row, mem, mem, row],
        out_specs=row,
        out_shape=jax.ShapeDtypeStruct((b, s, BRANCH_W), BF16),
        compiler_params=pltpu.CompilerParams(dimension_semantics=("arbitrary", "arbitrary")),
        name="mem_attn",
    )(qm, mk, mv, szm)


def _merge_kernel(ga_ref, gb_ref, gm_ref, sg_ref, x_ref, wa_ref, wb_ref, wm_ref, wo_ref, nf_ref, y_ref):
    d = x_ref.shape[1]
    mixed = sg_ref[:, :d].astype(F32) * _dot(ga_ref[...], wa_ref[...])
    mixed = mixed + sg_ref[:, d:2 * d].astype(F32) * _dot(gb_ref[...], wb_ref[...])
    mixed = mixed + sg_ref[:, 2 * d:].astype(F32) * _dot(gm_ref[...], wm_ref[...])
    hsum = x_ref[...] + _dot(mixed.astype(BF16), wo_ref[...])
    y_ref[...] = hsum * lax.rsqrt(jnp.mean(hsum * hsum, axis=-1, keepdims=True) + RMS_EPS) * nf_ref[...]


def _merge(ga, gb, gm, sg, x, wa, wb, wm, wo, norm_final):
    b, s, d = x.shape
    tm = min(PROJ_ROWS, s)
    row = lambda width: pl.BlockSpec((None, tm, width), lambda bi, j: (bi, j, 0))
    full = lambda a: pl.BlockSpec(a.shape, lambda bi, j: (0, 0))
    return pl.pallas_call(
        _merge_kernel,
        grid=(b, s // tm),
        in_specs=[row(BRANCH_W), row(BRANCH_W), row(BRANCH_W), row(sg.shape[2]), row(d),
                  full(wa), full(wb), full(wm), full(wo), full(norm_final)],
        out_specs=row(d),
        out_shape=jax.ShapeDtypeStruct((b, s, d), F32),
        compiler_params=pltpu.CompilerParams(dimension_semantics=("arbitrary", "arbitrary"),
                                             vmem_limit_bytes=VMEM_LIMIT),
        name="merge",
    )(ga, gb, gm, sg, x, wa, wb, wm, wo, norm_final)


def _chunk_attn_sample_kernel(q_ref, k_ref, v_ref, bias_ref, sz_ref, o_ref):
    t = q_ref.shape[0]
    lane = lax.broadcasted_iota(jnp.int32, (t, SLAB), 1)
    for p in range(BRANCH_W // SLAB):
        cols = slice(p * SLAB, (p + 1) * SLAB)
        q2 = _split_pair(q_ref[:, cols])
        s = _dot_nt(q2, k_ref[:, cols]) + bias_ref[p]
        m = jnp.max(s, axis=-1, keepdims=True)
        pr = jnp.exp(s - m)
        l = jnp.sum(pr, axis=-1, keepdims=True)
        o = _dot(pr.astype(BF16), v_ref[:, cols]) * (1.0 / l)
        o = jnp.where(lane < HALF, o[:t], o[t:])
        o_ref[:, cols] = (o * sz_ref[:, cols].astype(F32)).astype(BF16)


def _chunk_attn_sample(qa, k_cat, v_cat, bias, sza):
    b, t, _ = qa.shape
    n_keys = k_cat.shape[1]
    row = pl.BlockSpec((None, t, BRANCH_W), lambda bi: (bi, 0, 0))
    keys = pl.BlockSpec((None, n_keys, BRANCH_W), lambda bi: (bi, 0, 0))
    return pl.pallas_call(
        _chunk_attn_sample_kernel,
        grid=(b,),
        in_specs=[row, keys, keys, pl.BlockSpec(bias.shape, lambda bi: (0, 0, 0)), row],
        out_specs=row,
        out_shape=jax.ShapeDtypeStruct((b, t, BRANCH_W), BF16),
        compiler_params=pltpu.CompilerParams(dimension_semantics=("arbitrary",)),
        name="chunk_attn_sample",
    )(qa, k_cat, v_cat, bias, sza)


def _diff_attn_sample_kernel(q_ref, kc_ref, vc_ref, kn_ref, vn_ref, nbias_ref, sz_ref,
                             lq1_ref, lk1_ref, lq2_ref, lk2_ref, subln_ref, o_ref, *, lam_init):
    t = q_ref.shape[0]
    lam = _diff_lambda(lq1_ref, lk1_ref, lq2_ref, lk2_ref, lam_init)
    q2 = _split_pair(q_ref[...])
    s_c = _dot_nt(q2, kc_ref[...].astype(BF16))
    s_n = _dot_nt(q2, kn_ref[...]) + nbias_ref[...]
    m = jnp.maximum(jnp.max(s_c, axis=-1, keepdims=True), jnp.max(s_n, axis=-1, keepdims=True))
    p_c = jnp.exp(s_c - m)
    p_n = jnp.exp(s_n - m)
    l = jnp.sum(p_c, axis=-1, keepdims=True) + jnp.sum(p_n, axis=-1, keepdims=True)
    o = _dot(p_c.astype(BF16), vc_ref[...].astype(BF16)) + _dot(p_n.astype(BF16), vn_ref[...])
    o = o * (1.0 / l)
    od = o[:t] - lam * o[t:]
    od = od * lax.rsqrt(jnp.mean(od * od, axis=-1, keepdims=True) + RMS_EPS) * subln_ref[...] * (1.0 - lam_init)
    o_ref[...] = (od * sz_ref[...].astype(F32)).astype(BF16)


def _diff_attn_sample(qb, cache_k, cache_v, k_new, v_new, nbias, szb, lq1, lk1, lq2, lk2, subln_row, lam_init):
    b, t, _ = qb.shape
    past = cache_k.shape[1]
    n_new = k_new.shape[1]
    n_head = BRANCH_W // SLAB
    tok = pl.BlockSpec((None, t, SLAB), lambda bi, hd: (bi, 0, hd))
    cache = pl.BlockSpec((None, past, SLAB), lambda bi, hd: (bi, 0, hd))
    new = pl.BlockSpec((None, n_new, SLAB), lambda bi, hd: (bi, 0, hd))
    small = lambda a: pl.BlockSpec(a.shape, lambda bi, hd: (0, 0))
    return pl.pallas_call(
        functools.partial(_diff_attn_sample_kernel, lam_init=lam_init),
        grid=(b, n_head),
        in_specs=[tok, cache, cache, new, new, small(nbias), tok,
                  small(lq1), small(lk1), small(lq2), small(lk2), small(subln_row)],
        out_specs=tok,
        out_shape=jax.ShapeDtypeStruct((b, t, BRANCH_W), BF16),
        compiler_params=pltpu.CompilerParams(dimension_semantics=("arbitrary", "arbitrary")),
        name="diff_attn_sample",
    )(qb, cache_k, cache_v, k_new, v_new, nbias, szb, lq1, lk1, lq2, lk2, subln_row)


def _rope_tables(pos):
    inv = 1.0 / (ROPE_THETA ** (jnp.arange(0, HALF, 2, dtype=F32) / HALF))
    ang = pos.astype(F32)[:, None] * inv[None, :]
    cos = jnp.cos(ang)
    sin = jnp.sin(ang)
    return jnp.tile(cos, (1, SLAB // (HALF // 2))), jnp.tile(jnp.concatenate([-sin, sin], axis=1), (1, SLAB // HALF))


def _prompt_bias_table(rel_bias):
    u = jnp.arange(BAND + A_PAST)[:, None]
    c = jnp.arange(2 * Q_BLK)[None, :]
    ql = c % Q_BLK
    head_in_pair = c // Q_BLK
    idx = jnp.clip(ql + A_PAST - u, -REL_CLIP, REL_CLIP) + REL_CLIP
    visible = (u < BAND) & jnp.where(ql < CHUNK, u < A_PAST + CHUNK, u >= CHUNK)
    heads = 2 * jnp.arange(rel_bias.shape[0] // 2)[:, None, None] + head_in_pair[None]
    return jnp.where(visible[None], rel_bias[heads, idx[None]], NEG_INF).astype(F32)


def _sample_bias_table(rel_bias, p_len, t, n_keys):
    r = jnp.arange(2 * t)[:, None]
    j = jnp.arange(n_keys)[None, :]
    idx = jnp.clip(p_len + (r % t) - j, -REL_CLIP, REL_CLIP) + REL_CLIP
    heads = 2 * jnp.arange(rel_bias.shape[0] // 2)[:, None, None] + (r // t)[None]
    return jnp.where((j < p_len + t)[None], rel_bias[heads, idx[None]], NEG_INF).astype(F32)


def _round_up(n, m):
    return (n + m - 1) // m * m


def kernel(x_prompt, x_sample, cache_a_k, cache_a_v, cache_b_k, cache_b_v, cache_mem_k, cache_mem_v, mem_prompt, norm_in, w_in, rel_bias, lambda_q1, lambda_k1, lambda_q2, lambda_k2, subln, norm_mem, w_mem_kv, w_branch_a, w_branch_b, w_branch_m, w_out, norm_final):
    depth = w_in.shape[0]
    assert depth == 1, "kernels are written for the single-layer step"
    bsz, seq, d_model = x_prompt.shape
    dec_b, dec_t, _ = x_sample.shape
    past = cache_b_k.shape[2]
    a_cache = cache_a_k.shape[2]
    n_mem = mem_prompt.shape[1]
    keep = min(A_PAST, seq)
    lam_init = 0.8 - 0.6 * math.exp(-0.3 * 0)
    l = 0

    w_in_b = w_in[l].astype(BF16)
    w_mem_b = w_mem_kv[l].astype(BF16)
    wa, wb, wm, wo = (w[l].astype(BF16) for w in (w_branch_a, w_branch_b, w_branch_m, w_out))
    g_in = norm_in[l][None, :]
    g_mem = norm_mem[l][None, :]
    g_final = norm_final[None, :]
    lq1, lk1, lq2, lk2 = (v[l][None, :] for v in (lambda_q1, lambda_k1, lambda_q2, lambda_k2))

    cos_p, sin_p = _rope_tables(jnp.arange(seq))
    (qa, ka, va_t, ka_keep, va_keep, qb, kb, kb_f32, vb_t, vb_f32, qm, sza, szb, szm, sg) = _in_proj(
        x_prompt, g_in, w_in_b, cos_p, sin_p, keep=keep, transposed_v=True)
    ga = _chunk_attn_prompt(qa, ka, va_t, _prompt_bias_table(rel_bias[l]), sza)
    gb = _diff_attn_prompt(qb, kb, vb_t, szb, lq1, lk1, lq2, lk2, subln[l][:, None], lam_init)
    mk_f32, mv_f32, mk, mv = _mem_kv(mem_prompt, g_mem, w_mem_b)
    gm = _mem_attn(qm, mk, mv, szm)
    y_prompt = _merge(ga, gb, gm, sg, x_prompt, wa, wb, wm, wo, g_final)

    n_tok = dec_b * dec_t
    pos_s = past + jnp.tile(jnp.arange(dec_t), dec_b)
    cos_s, sin_s = _rope_tables(pos_s)
    xs = x_sample.reshape(1, n_tok, d_model)
    (qa_s, ka_s, va_s, ka_s32, va_s32, qb_s, kb_s, kb_s32, vb_s, vb_s32, qm_s, sza_s, szb_s, szm_s, sg_s) = _in_proj(
        xs, g_in, w_in_b, cos_s, sin_s, keep=n_tok, transposed_v=False)
    per_req = lambda a: a.reshape(dec_b, dec_t, a.shape[-1])

    n_keys = _round_up(a_cache + dec_t, SLAB)
    pad_a = ((0, 0), (0, n_keys - a_cache - dec_t), (0, 0))
    k_cat = jnp.pad(jnp.concatenate([cache_a_k[l].reshape(dec_b, a_cache, BRANCH_W).astype(BF16), per_req(ka_s)], axis=1), pad_a)
    v_cat = jnp.pad(jnp.concatenate([cache_a_v[l].reshape(dec_b, a_cache, BRANCH_W).astype(BF16), per_req(va_s)], axis=1), pad_a)
    ga_s = _chunk_attn_sample(per_req(qa_s), k_cat, v_cat, _sample_bias_table(rel_bias[l], a_cache, dec_t, n_keys),
                              per_req(sza_s))

    n_new = _round_up(dec_t, SLAB)
    pad_b = ((0, 0), (0, n_new - dec_t), (0, 0))
    nbias = jnp.where(jnp.arange(n_new) < dec_t, 0.0, NEG_INF).astype(F32)[None, :]
    gb_s = _diff_attn_sample(per_req(qb_s), cache_b_k[l].reshape(dec_b, past, BRANCH_W),
                             cache_b_v[l].reshape(dec_b, past, BRANCH_W),
                             jnp.pad(per_req(kb_s), pad_b), jnp.pad(per_req(vb_s), pad_b), nbias, per_req(szb_s),
                             lq1, lk1, lq2, lk2, subln[l][None, :], lam_init)
    gm_s = _mem_attn(per_req(qm_s), cache_mem_k[l].reshape(dec_b, n_mem, BRANCH_W),
                     cache_mem_v[l].reshape(dec_b, n_mem, BRANCH_W), per_req(szm_s))
    flat = lambda a: a.reshape(1, n_tok, a.shape[-1])
    y_sample = _merge(flat(ga_s), flat(gb_s), flat(gm_s), sg_s, xs, wa, wb, wm, wo, g_final).reshape(x_sample.shape)

    a_heads, a_dim = cache_a_k.shape[3:]
    bk_heads, bk_dim = cache_b_k.shape[3:]
    bv_heads, bv_dim = cache_b_v.shape[3:]
    m_heads, m_dim = cache_mem_k.shape[3:]
    return (y_prompt, y_sample,
            ka_keep.reshape(1, bsz, keep, a_heads, a_dim), va_keep.reshape(1, bsz, keep, a_heads, a_dim),
            kb_f32.reshape(1, bsz, seq, bk_heads, bk_dim), vb_f32.reshape(1, bsz, seq, bv_heads, bv_dim),
            mk_f32.reshape(1, bsz, n_mem, m_heads, m_dim), mv_f32.reshape(1, bsz, n_mem, m_heads, m_dim),
            ka_s32.reshape(1, dec_b, dec_t, a_heads, a_dim), va_s32.reshape(1, dec_b, dec_t, a_heads, a_dim),
            kb_s32.reshape(1, dec_b, dec_t, bk_heads, bk_dim), vb_s32.reshape(1, dec_b, dec_t, bv_heads, bv_dim))
```

```python
import functools
import math

import jax
import jax.numpy as jnp
from jax import lax
from jax.experimental import pallas as pl
from jax.experimental.pallas import tpu as pltpu

F32 = jnp.float32
BF16 = jnp.bfloat16

CHUNK = 64
CHUNK_SHIFT = 6
A_PAST = 8 * CHUNK
REL_CLIP = 128
BRANCH_W = 512
SLAB = 128
HALF = 64
ROPE_THETA = 10000.0
RMS_EPS = 1e-6
NEG_INF = -1e30

PROJ_ROWS = 512
VT_BLK = 256
KV_STEP = 512
Q_BLK = 128
BAND = A_PAST + Q_BLK
BAND_CONST = A_PAST - REL_CLIP
DQ_BLK = 512
LOG2E = math.log2(math.e)
CHUNK_UNROLL = 4
assert DQ_BLK == KV_STEP

VMEM_LIMIT = 56 * 1024 * 1024


def _dot(a, b):
    return jnp.dot(a, b, preferred_element_type=F32)


def _dot_nt(a, b):
    return lax.dot_general(a, b, (((1,), (1,)), ((), ())), preferred_element_type=F32)


def _silu(z):
    return z * (1.0 / (1.0 + jnp.exp(-z)))


def _sigmoid(z):
    return 1.0 / (1.0 + jnp.exp(-z))


def _split_pair(q):
    lane = lax.broadcasted_iota(jnp.int32, q.shape, 1)
    zero = jnp.zeros_like(q)
    return jnp.concatenate([jnp.where(lane < HALF, q, zero), jnp.where(lane >= HALF, q, zero)], axis=0)


def _diff_lambda(lq1_ref, lk1_ref, lq2_ref, lk2_ref, lam_init):
    e1 = jnp.exp(jnp.sum(lq1_ref[...] * lk1_ref[...], axis=-1, keepdims=True))
    e2 = jnp.exp(jnp.sum(lq2_ref[...] * lk2_ref[...], axis=-1, keepdims=True))
    return e1 - e2 + lam_init


def _in_proj_kernel(x_ref, g_ref, w_ref, cos_ref, sin_ref,
                    qa_ref, ka_ref, va_ref, kaw_ref, vaw_ref,
                    qb_ref, kb_ref, kbf_ref, vb_ref, vbf_ref,
                    qm_ref, sza_ref, szb_ref, szm_ref, sg_ref, *, transposed_v, mem_scale):
    tm = x_ref.shape[0]
    x = x_ref[...]
    h = (x * lax.rsqrt(jnp.mean(x * x, axis=-1, keepdims=True) + RMS_EPS) * g_ref[...]).astype(BF16)

    def proj(idx):
        return _dot(h, w_ref[:, idx * BRANCH_W:(idx + 1) * BRANCH_W])

    cos = cos_ref[...]
    sin = sin_ref[...]
    lane = lax.broadcasted_iota(jnp.int32, (tm, SLAB), 1)
    first_half = (lane & (HALF // 2)) == 0

    def rope_slabs(acc):
        out = []
        for j in range(BRANCH_W // SLAB):
            y = acc[:, j * SLAB:(j + 1) * SLAB]
            partner = jnp.where(first_half, pltpu.roll(y, SLAB - HALF // 2, 1), pltpu.roll(y, HALF // 2, 1))
            out.append(y * cos + partner * sin)
        return out

    qa_ref[...] = (proj(0) * (LOG2E * HALF ** -0.5)).astype(BF16)
    ak = proj(1)
    ka_ref[...] = ak.astype(BF16)
    kaw_ref[...] = ak
    av = proj(2)
    vaw_ref[...] = av
    if transposed_v:
        for p in range(BRANCH_W // SLAB):
            for i in range(tm // SLAB):
                va_ref[p, i] = av[i * SLAB:(i + 1) * SLAB, p * SLAB:(p + 1) * SLAB].T.astype(BF16)
    else:
        va_ref[...] = av.astype(BF16)
    sza_ref[...] = _silu(proj(3)).astype(BF16)

    for j, y in enumerate(rope_slabs(proj(4))):
        qb_ref[:, j * SLAB:(j + 1) * SLAB] = (y * (LOG2E * HALF ** -0.5)).astype(BF16)
    for j, y in enumerate(rope_slabs(proj(5))):
        kbf_ref[:, j * SLAB:(j + 1) * SLAB] = y
        kb_ref[:, j * SLAB:(j + 1) * SLAB] = y.astype(BF16)
    bv = proj(6)
    vbf_ref[...] = bv
    if transposed_v:
        for hd in range(BRANCH_W // SLAB):
            for i in range(tm // VT_BLK):
                vb_ref[hd, i] = bv[i * VT_BLK:(i + 1) * VT_BLK, hd * SLAB:(hd + 1) * SLAB].T.astype(BF16)
    else:
        vb_ref[...] = bv.astype(BF16)
    szb_ref[...] = _silu(proj(7)).astype(BF16)

    qm_ref[...] = (proj(8) * mem_scale).astype(BF16)
    szm_ref[...] = _silu(proj(9)).astype(BF16)
    for j in range(sg_ref.shape[1] // BRANCH_W):
        sg_ref[:, j * BRANCH_W:(j + 1) * BRANCH_W] = _sigmoid(proj(10 + j)).astype(BF16)


def _in_proj(x, norm_g, w_bf16, cos_tab, sin_tab, *, keep, transposed_v):
    b, s, d = x.shape
    d_in = w_bf16.shape[1]
    tm = min(PROJ_ROWS, s)
    nt = s // tm
    first_kept = (s - keep) // tm
    n_gate = d_in - 10 * BRANCH_W

    row = lambda bi, j: (bi, j, 0)
    kept = lambda bi, j: (bi, jnp.maximum(j - first_kept, 0), 0)
    act = lambda dt: jax.ShapeDtypeStruct((b, s, BRANCH_W), dt)
    act_spec = pl.BlockSpec((None, tm, BRANCH_W), row)
    win = jax.ShapeDtypeStruct((b, keep, BRANCH_W), F32)
    win_spec = pl.BlockSpec((None, tm, BRANCH_W), kept)
    if transposed_v:
        va = jax.ShapeDtypeStruct((b, BRANCH_W // SLAB, s // SLAB, SLAB, SLAB), BF16)
        va_spec = pl.BlockSpec((None, BRANCH_W // SLAB, tm // SLAB, SLAB, SLAB), lambda bi, j: (bi, 0, j, 0, 0))
        vb = jax.ShapeDtypeStruct((b, BRANCH_W // SLAB, s // VT_BLK, SLAB, VT_BLK), BF16)
        vb_spec = pl.BlockSpec((None, BRANCH_W // SLAB, tm // VT_BLK, SLAB, VT_BLK), lambda bi, j: (bi, 0, j, 0, 0))
    else:
        va, va_spec, vb, vb_spec = act(BF16), act_spec, act(BF16), act_spec

    out_shape = [act(BF16), act(BF16), va, win, win,
                 act(BF16), act(BF16), act(F32), vb, act(F32),
                 act(BF16), act(BF16), act(BF16), act(BF16),
                 jax.ShapeDtypeStruct((b, s, n_gate), BF16)]
    out_specs = [act_spec, act_spec, va_spec, win_spec, win_spec,
                 act_spec, act_spec, act_spec, vb_spec, act_spec,
                 act_spec, act_spec, act_spec, act_spec,
                 pl.BlockSpec((None, tm, n_gate), row)]
    return pl.pallas_call(
        functools.partial(_in_proj_kernel, transposed_v=transposed_v, mem_scale=LOG2E * SLAB ** -0.5),
        grid=(b, nt),
        in_specs=[pl.BlockSpec((None, tm, d), row),
                  pl.BlockSpec((1, d), lambda bi, j: (0, 0)),
                  pl.BlockSpec((d, d_in), lambda bi, j: (0, 0), pipeline_mode=pl.Buffered(1)),
                  pl.BlockSpec((tm, SLAB), lambda bi, j: (j, 0)),
                  pl.BlockSpec((tm, SLAB), lambda bi, j: (j, 0))],
        out_specs=out_specs,
        out_shape=out_shape,
        compiler_params=pltpu.CompilerParams(dimension_semantics=("arbitrary", "arbitrary"),
                                             vmem_limit_bytes=VMEM_LIMIT),
        name="in_proj",
    )(x, norm_g, w_bf16, cos_tab, sin_tab)


def _chunk_attn_prompt_kernel(q_ref, k_ref, vt_ref, bias_ref, sz_ref, o_ref):
    s_len = q_ref.shape[0]

    def q_block(g, carry):
        q0 = pl.multiple_of(g * Q_BLK, Q_BLK)
        q2 = _split_pair(q_ref[pl.ds(q0, Q_BLK), :])
        start = g * Q_BLK - A_PAST
        start_c = jnp.maximum(start, 0)
        off = pl.multiple_of(start_c - start, SLAB)
        start_c = pl.multiple_of(start_c, SLAB)
        kb = k_ref[pl.ds(start_c, BAND), :]
        s = _dot_nt(kb, q2) + bias_ref[pl.ds(off, BAND), :]
        m = jnp.max(s, axis=0, keepdims=True)
        p = jnp.exp2(s - m)
        l = jnp.sum(p, axis=0, keepdims=True)
        pb = p.astype(BF16)
        blk0 = start_c // SLAB
        acc = jnp.zeros((SLAB, 2 * Q_BLK), F32)
        for i in range(BAND // SLAB):
            acc = acc + _dot(vt_ref[blk0 + i], pb[i * SLAB:(i + 1) * SLAB, :])
        acc = acc * (1.0 / l)
        o_t = jnp.concatenate([acc[:HALF, :Q_BLK], acc[HALF:, Q_BLK:]], axis=0)
        o = o_t.T * sz_ref[pl.ds(q0, Q_BLK), :].astype(F32)
        o_ref[pl.ds(q0, Q_BLK), :] = o.astype(BF16)
        return carry

    lax.fori_loop(0, s_len // Q_BLK, q_block, 0, unroll=CHUNK_UNROLL)


def _chunk_attn_prompt(qa, ka, va_t, bias_t, sza):
    b, s, _ = qa.shape
    n_pair = BRANCH_W // SLAB
    slab = pl.BlockSpec((None, s, SLAB), lambda bi, p: (bi, 0, p))
    return pl.pallas_call(
        _chunk_attn_prompt_kernel,
        grid=(b, n_pair),
        in_specs=[slab, slab,
                  pl.BlockSpec((None, None, s // SLAB, SLAB, SLAB), lambda bi, p: (bi, p, 0, 0, 0)),
                  pl.BlockSpec((None, BAND + A_PAST, 2 * Q_BLK), lambda bi, p: (p, 0, 0)),
                  slab],
        out_specs=slab,
        out_shape=jax.ShapeDtypeStruct((b, s, BRANCH_W), BF16),
        compiler_params=pltpu.CompilerParams(dimension_semantics=("arbitrary", "arbitrary"),
                                             vmem_limit_bytes=VMEM_LIMIT),
        name="chunk_attn_prompt",
    )(qa, ka, va_t, bias_t, sza)


def _diff_attn_prompt_kernel(q_ref, k_ref, vt_ref, sz_ref, lq1_ref, lk1_ref, lq2_ref, lk2_ref, subln_ref,
                             o_ref, *, lam_init):
    s_len = q_ref.shape[0]
    lam = _diff_lambda(lq1_ref, lk1_ref, lq2_ref, lk2_ref, lam_init)
    sub_blocks = KV_STEP // VT_BLK

    def q_block(qi, carry):
        q0 = pl.multiple_of(qi * DQ_BLK, DQ_BLK)
        q2 = _split_pair(q_ref[pl.ds(q0, DQ_BLK), :])

        def kv_step(kv, state, masked):
            m, l, acc = state
            k0 = pl.multiple_of(kv * KV_STEP, KV_STEP)
            s = _dot_nt(k_ref[pl.ds(k0, KV_STEP), :], q2)
            if masked:
                k_chunk = jnp.right_shift(k0 + lax.broadcasted_iota(jnp.int32, s.shape, 0), CHUNK_SHIFT)
                q_chunk = jnp.right_shift(q0 + (lax.broadcasted_iota(jnp.int32, s.shape, 1) & (DQ_BLK - 1)), CHUNK_SHIFT)
                s = jnp.where(k_chunk <= q_chunk, s, NEG_INF)
            m_new = jnp.maximum(m, jnp.max(s, axis=0, keepdims=True))
            alpha = jnp.exp2(m - m_new)
            p = jnp.exp2(s - m_new)
            l = alpha * l + jnp.sum(p, axis=0, keepdims=True)
            pb = p.astype(BF16)
            pv = _dot(vt_ref[sub_blocks * kv], pb[:VT_BLK, :])
            for i in range(1, sub_blocks):
                pv = pv + _dot(vt_ref[sub_blocks * kv + i], pb[i * VT_BLK:(i + 1) * VT_BLK, :])
            return m_new, l, alpha * acc + pv

        n_full = (qi * DQ_BLK) // KV_STEP
        state = (jnp.full((1, 2 * DQ_BLK), NEG_INF, F32), jnp.zeros((1, 2 * DQ_BLK), F32),
                 jnp.zeros((SLAB, 2 * DQ_BLK), F32))
        state = lax.fori_loop(0, n_full, lambda kv, st: kv_step(kv, st, False), state)
        _, l, acc = kv_step(n_full, state, True)

        acc = acc * (1.0 / l)
        o_t = acc[:, :DQ_BLK] - lam * acc[:, DQ_BLK:]
        o_t = o_t * lax.rsqrt(jnp.mean(o_t * o_t, axis=0, keepdims=True) + RMS_EPS)
        o_t = o_t * subln_ref[...] * (1.0 - lam_init)
        o = o_t.T * sz_ref[pl.ds(q0, DQ_BLK), :].astype(F32)
        o_ref[pl.ds(q0, DQ_BLK), :] = o.astype(BF16)
        return carry

    lax.fori_loop(0, s_len // DQ_BLK, q_block, 0)


def _diff_attn_prompt(qb, kb, vb_t, szb, lq1, lk1, lq2, lk2, subln_col, lam_init):
    b, s, _ = qb.shape
    n_head = BRANCH_W // SLAB
    slab = pl.BlockSpec((None, s, SLAB), lambda bi, hd: (bi, 0, hd))
    small = lambda shape: pl.BlockSpec(shape, lambda bi, hd: (0, 0))
    return pl.pallas_call(
        functools.partial(_diff_attn_prompt_kernel, lam_init=lam_init),
        grid=(b, n_head),
        in_specs=[slab, slab,
                  pl.BlockSpec((None, None, s // VT_BLK, SLAB, VT_BLK), lambda bi, hd: (bi, hd, 0, 0, 0)),
                  slab,
                  small(lq1.shape), small(lk1.shape), small(lq2.shape), small(lk2.shape), small(subln_col.shape)],
        out_specs=slab,
        out_shape=jax.ShapeDtypeStruct((b, s, BRANCH_W), BF16),
        compiler_params=pltpu.CompilerParams(dimension_semantics=("arbitrary", "arbitrary"),
                                             vmem_limit_bytes=VMEM_LIMIT),
        name="diff_attn_prompt",
    )(qb, kb, vb_t, szb, lq1, lk1, lq2, lk2, subln_col)


def _mem_kv_kernel(mem_ref, g_ref, w_ref, kf_ref, vf_ref, kb_ref, vb_ref):
    x = mem_ref[...]
    h = (x * lax.rsqrt(jnp.mean(x * x, axis=-1, keepdims=True) + RMS_EPS) * g_ref[...]).astype(BF16)
    k = _dot(h, w_ref[:, :BRANCH_W])
    v = _dot(h, w_ref[:, BRANCH_W:])
    kf_ref[...] = k
    vf_ref[...] = v
    kb_ref[...] = k.astype(BF16)
    vb_ref[...] = v.astype(BF16)


def _mem_kv(mem, norm_g, w_bf16):
    b, n, d = mem.shape
    blk = lambda width: pl.BlockSpec((None, n, width), lambda bi: (bi, 0, 0))
    out = lambda dt: jax.ShapeDtypeStruct((b, n, BRANCH_W), dt)
    return pl.pallas_call(
        _mem_kv_kernel,
        grid=(b,),
        in_specs=[blk(d), pl.BlockSpec((1, d), lambda bi: (0, 0)),
                  pl.BlockSpec((d, 2 * BRANCH_W), lambda bi: (0, 0))],
        out_specs=[blk(BRANCH_W)] * 4,
        out_shape=[out(F32), out(F32), out(BF16), out(BF16)],
        compiler_params=pltpu.CompilerParams(dimension_semantics=("arbitrary",)),
        name="mem_kv",
    )(mem, norm_g, w_bf16)


def _mem_attn_kernel(q_ref, k_ref, v_ref, sz_ref, o_ref):
    for hd in range(BRANCH_W // SLAB):
        cols = slice(hd * SLAB, (hd + 1) * SLAB)
        s = _dot_nt(q_ref[:, cols], k_ref[:, cols].astype(BF16))
        m = jnp.max(s, axis=-1, keepdims=True)
        p = jnp.exp2(s - m)
        l = jnp.sum(p, axis=-1, keepdims=True)
        o = _dot(p.astype(BF16), v_ref[:, cols].astype(BF16)) * (1.0 / l)
        o_ref[:, cols] = (o * sz_ref[:, cols].astype(F32)).astype(BF16)


def _mem_attn(qm, mk, mv, szm):
    b, s, _ = qm.shape
    n = mk.shape[1]
    tq = min(PROJ_ROWS, s)
    row = pl.BlockSpec((None, tq, BRANCH_W), lambda bi, j: (bi, j, 0))
    mem = pl.BlockSpec((None, n, BRANCH_W), lambda bi, j: (bi, 0, 0))
    return pl.pallas_call(
        _mem_attn_kernel,
        grid=(b, s // tq),
        in_specs=[row, mem, mem, row],
        out_specs=row,
        out_shape=jax.ShapeDtypeStruct((b, s, BRANCH_W), BF16),
        compiler_params=pltpu.CompilerParams(dimension_semantics=("arbitrary", "arbitrary")),
        name="mem_attn",
    )(qm, mk, mv, szm)


def _merge_kernel(ga_ref, gb_ref, gm_ref, sg_ref, x_ref, wa_ref, wb_ref, wm_ref, wo_ref, nf_ref, y_ref):
    d = x_ref.shape[1]
    mixed = sg_ref[:, :d].astype(F32) * _dot(ga_ref[...], wa_ref[...])
    mixed = mixed + sg_ref[:, d:2 * d].astype(F32) * _dot(gb_ref[...], wb_ref[...])
    mixed = mixed + sg_ref[:, 2 * d:].astype(F32) * _dot(gm_ref[...], wm_ref[...])
    hsum = x_ref[...] + _dot(mixed.astype(BF16), wo_ref[...])
    y_ref[...] = hsum * lax.rsqrt(jnp.mean(hsum * hsum, axis=-1, keepdims=True) + RMS_EPS) * nf_ref[...]


def _merge(ga, gb, gm, sg, x, wa, wb, wm, wo, norm_final):
    b, s, d = x.shape
    tm = min(PROJ_ROWS, s)
    row = lambda width: pl.BlockSpec((None, tm, width), lambda bi, j: (bi, j, 0))
    full = lambda a: pl.BlockSpec(a.shape, lambda bi, j: (0, 0))
    return pl.pallas_call(
        _merge_kernel,
        grid=(b, s // tm),
        in_specs=[row(BRANCH_W), row(BRANCH_W), row(BRANCH_W), row(sg.shape[2]), row(d),
                  full(wa), full(wb), full(wm), full(wo), full(norm_final)],
        out_specs=row(d),
        out_shape=jax.ShapeDtypeStruct((b, s, d), F32),
        compiler_params=pltpu.CompilerParams(dimension_semantics=("arbitrary", "arbitrary"),
                                             vmem_limit_bytes=VMEM_LIMIT),
        name="merge",
    )(ga, gb, gm, sg, x, wa, wb, wm, wo, norm_final)


def _chunk_attn_sample_kernel(q_ref, k_ref, v_ref, bias_ref, sz_ref, o_ref):
    t = q_ref.shape[0]
    lane = lax.broadcasted_iota(jnp.int32, (t, SLAB), 1)
    for p in range(BRANCH_W // SLAB):
        cols = slice(p * SLAB, (p + 1) * SLAB)
        q2 = _split_pair(q_ref[:, cols])
        s = _dot_nt(q2, k_ref[:, cols]) + bias_ref[p]
        m = jnp.max(s, axis=-1, keepdims=True)
        pr = jnp.exp2(s - m)
        l = jnp.sum(pr, axis=-1, keepdims=True)
        o = _dot(pr.astype(BF16), v_ref[:, cols]) * (1.0 / l)
        o = jnp.where(lane < HALF, o[:t], o[t:])
        o_ref[:, cols] = (o * sz_ref[:, cols].astype(F32)).astype(BF16)


def _chunk_attn_sample(qa, k_cat, v_cat, bias, sza):
    b, t, _ = qa.shape
    n_keys = k_cat.shape[1]
    row = pl.BlockSpec((None, t, BRANCH_W), lambda bi: (bi, 0, 0))
    keys = pl.BlockSpec((None, n_keys, BRANCH_W), lambda bi: (bi, 0, 0))
    return pl.pallas_call(
        _chunk_attn_sample_kernel,
        grid=(b,),
        in_specs=[row, keys, keys, pl.BlockSpec(bias.shape, lambda bi: (0, 0, 0)), row],
        out_specs=row,
        out_shape=jax.ShapeDtypeStruct((b, t, BRANCH_W), BF16),
        compiler_params=pltpu.CompilerParams(dimension_semantics=("arbitrary",)),
        name="chunk_attn_sample",
    )(qa, k_cat, v_cat, bias, sza)


def _diff_attn_sample_kernel(q_ref, kc_ref, vc_ref, kn_ref, vn_ref, nbias_ref, sz_ref,
                             lq1_ref, lk1_ref, lq2_ref, lk2_ref, subln_ref, o_ref, *, lam_init):
    t = q_ref.shape[0]
    lam = _diff_lambda(lq1_ref, lk1_ref, lq2_ref, lk2_ref, lam_init)
    q2 = _split_pair(q_ref[...])
    s_c = _dot_nt(q2, kc_ref[...].astype(BF16))
    s_n = _dot_nt(q2, kn_ref[...]) + nbias_ref[...]
    m = jnp.maximum(jnp.max(s_c, axis=-1, keepdims=True), jnp.max(s_n, axis=-1, keepdims=True))
    p_c = jnp.exp2(s_c - m)
    p_n = jnp.exp2(s_n - m)
    l = jnp.sum(p_c, axis=-1, keepdims=True) + jnp.sum(p_n, axis=-1, keepdims=True)
    o = _dot(p_c.astype(BF16), vc_ref[...].astype(BF16)) + _dot(p_n.astype(BF16), vn_ref[...])
    o = o * (1.0 / l)
    od = o[:t] - lam * o[t:]
    od = od * lax.rsqrt(jnp.mean(od * od, axis=-1, keepdims=True) + RMS_EPS) * subln_ref[...] * (1.0 - lam_init)
    o_ref[...] = (od * sz_ref[...].astype(F32)).astype(BF16)


def _diff_attn_sample(qb, cache_k, cache_v, k_new, v_new, nbias, szb, lq1, lk1, lq2, lk2, subln_row, lam_init):
    b, t, _ = qb.shape
    past = cache_k.shape[1]
    n_new = k_new.shape[1]
    n_head = BRANCH_W // SLAB
    tok = pl.BlockSpec((None, t, SLAB), lambda bi, hd: (bi, 0, hd))
    cache = pl.BlockSpec((None, past, SLAB), lambda bi, hd: (bi, 0, hd))
    new = pl.BlockSpec((None, n_new, SLAB), lambda bi, hd: (bi, 0, hd))
    small = lambda a: pl.BlockSpec(a.shape, lambda bi, hd: (0, 0))
    return pl.pallas_call(
        functools.partial(_diff_attn_sample_kernel, lam_init=lam_init),
        grid=(b, n_head),
        in_specs=[tok, cache, cache, new, new, small(nbias), tok,
                  small(lq1), small(lk1), small(lq2), small(lk2), small(subln_row)],
        out_specs=tok,
        out_shape=jax.ShapeDtypeStruct((b, t, BRANCH_W), BF16),
        compiler_params=pltpu.CompilerParams(dimension_semantics=("arbitrary", "arbitrary")),
        name="diff_attn_sample",
    )(qb, cache_k, cache_v, k_new, v_new, nbias, szb, lq1, lk1, lq2, lk2, subln_row)


def _rope_tables(pos):
    inv = 1.0 / (ROPE_THETA ** (jnp.arange(0, HALF, 2, dtype=F32) / HALF))
    ang = pos.astype(F32)[:, None] * inv[None, :]
    cos = jnp.cos(ang)
    sin = jnp.sin(ang)
    return jnp.tile(cos, (1, SLAB // (HALF // 2))), jnp.tile(jnp.concatenate([-sin, sin], axis=1), (1, SLAB // HALF))


def _bias_lookup(rb_ref, pair, dist, head_in_pair):
    idx = jnp.clip(dist, -REL_CLIP, REL_CLIP) + REL_CLIP

    def entry(k, acc):
        v = jnp.where(head_in_pair == 1, rb_ref[2 * pair + 1, k], rb_ref[2 * pair, k])
        return jnp.where(idx == k, v, acc)

    return LOG2E * lax.fori_loop(0, 2 * REL_CLIP + 1, entry, jnp.zeros(dist.shape, F32))


def _prompt_bias_kernel(rb_ref, o_ref):
    pair = pl.program_id(0)

    def rows(u0, n):
        u = u0 + lax.broadcasted_iota(jnp.int32, (n, 2 * Q_BLK), 0)
        c = lax.broadcasted_iota(jnp.int32, (n, 2 * Q_BLK), 1)
        ql = c & (Q_BLK - 1)
        visible = ((ql < CHUNK) & (u < A_PAST + CHUNK)) | ((ql >= CHUNK) & (u >= CHUNK))
        return ql + A_PAST - u, jnp.where(c >= Q_BLK, 1, 0), visible

    dist, head_in_pair, visible = rows(BAND_CONST, BAND - BAND_CONST)
    o_ref[BAND_CONST:BAND, :] = jnp.where(visible, _bias_lookup(rb_ref, pair, dist, head_in_pair), NEG_INF)
    _, head_in_pair, visible = rows(0, BAND_CONST)
    far = LOG2E * jnp.where(head_in_pair == 1, rb_ref[2 * pair + 1, 2 * REL_CLIP], rb_ref[2 * pair, 2 * REL_CLIP])
    o_ref[:BAND_CONST, :] = jnp.where(visible, far, NEG_INF)
    o_ref[BAND:, :] = jnp.full((o_ref.shape[0] - BAND, 2 * Q_BLK), NEG_INF, F32)


def _prompt_bias_table(rel_bias):
    n_pair = rel_bias.shape[0] // 2
    rows = BAND + A_PAST
    return pl.pallas_call(
        _prompt_bias_kernel,
        grid=(n_pair,),
        in_specs=[pl.BlockSpec(memory_space=pltpu.SMEM)],
        out_specs=pl.BlockSpec((None, rows, 2 * Q_BLK), lambda p: (p, 0, 0)),
        out_shape=jax.ShapeDtypeStruct((n_pair, rows, 2 * Q_BLK), F32),
        compiler_params=pltpu.CompilerParams(dimension_semantics=("arbitrary",)),
        name="prompt_bias",
    )(rel_bias)


def _sample_bias_kernel(rb_ref, o_ref, *, p_len, t):
    pair = pl.program_id(0)
    r = lax.broadcasted_iota(jnp.int32, o_ref.shape, 0)
    j = lax.broadcasted_iota(jnp.int32, o_ref.shape, 1)
    head_in_pair = jnp.where(r >= t, 1, 0)
    dist = p_len + r - t * head_in_pair - j
    o_ref[...] = jnp.where(j < p_len + t, _bias_lookup(rb_ref, pair, dist, head_in_pair), NEG_INF)


def _sample_bias_table(rel_bias, p_len, t, n_keys):
    n_pair = rel_bias.shape[0] // 2
    return pl.pallas_call(
        functools.partial(_sample_bias_kernel, p_len=p_len, t=t),
        grid=(n_pair,),
        in_specs=[pl.BlockSpec(memory_space=pltpu.SMEM)],
        out_specs=pl.BlockSpec((None, 2 * t, n_keys), lambda p: (p, 0, 0)),
        out_shape=jax.ShapeDtypeStruct((n_pair, 2 * t, n_keys), F32),
        compiler_params=pltpu.CompilerParams(dimension_semantics=("arbitrary",)),
        name="sample_bias",
    )(rel_bias)


def _round_up(n, m):
    return (n + m - 1) // m * m


def kernel(x_prompt, x_sample, cache_a_k, cache_a_v, cache_b_k, cache_b_v, cache_mem_k, cache_mem_v, mem_prompt, norm_in, w_in, rel_bias, lambda_q1, lambda_k1, lambda_q2, lambda_k2, subln, norm_mem, w_mem_kv, w_branch_a, w_branch_b, w_branch_m, w_out, norm_final):
    depth = w_in.shape[0]
    assert depth == 1, "kernels are written for the single-layer step"
    bsz, seq, d_model = x_prompt.shape
    dec_b, dec_t, _ = x_sample.shape
    past = cache_b_k.shape[2]
    a_cache = cache_a_k.shape[2]
    n_mem = mem_prompt.shape[1]
    keep = min(A_PAST, seq)
    lam_init = 0.8 - 0.6 * math.exp(-0.3 * 0)
    l = 0

    w_in_b = w_in[l].astype(BF16)
    w_mem_b = w_mem_kv[l].astype(BF16)
    wa, wb, wm, wo = (w[l].astype(BF16) for w in (w_branch_a, w_branch_b, w_branch_m, w_out))
    g_in = norm_in[l][None, :]
    g_mem = norm_mem[l][None, :]
    g_final = norm_final[None, :]
    lq1, lk1, lq2, lk2 = (v[l][None, :] for v in (lambda_q1, lambda_k1, lambda_q2, lambda_k2))

    cos_p, sin_p = _rope_tables(jnp.arange(seq))
    (qa, ka, va_t, ka_keep, va_keep, qb, kb, kb_f32, vb_t, vb_f32, qm, sza, szb, szm, sg) = _in_proj(
        x_prompt, g_in, w_in_b, cos_p, sin_p, keep=keep, transposed_v=True)
    ga = _chunk_attn_prompt(qa, ka, va_t, _prompt_bias_table(rel_bias[l]), sza)
    gb = _diff_attn_prompt(qb, kb, vb_t, szb, lq1, lk1, lq2, lk2, subln[l][:, None], lam_init)
    mk_f32, mv_f32, mk, mv = _mem_kv(mem_prompt, g_mem, w_mem_b)
    gm = _mem_attn(qm, mk, mv, szm)
    y_prompt = _merge(ga, gb, gm, sg, x_prompt, wa, wb, wm, wo, g_final)

    n_tok = dec_b * dec_t
    pos_s = past + jnp.tile(jnp.arange(dec_t), dec_b)
    cos_s, sin_s = _rope_tables(pos_s)
    xs = x_sample.reshape(1, n_tok, d_model)
    (qa_s, ka_s, va_s, ka_s32, va_s32, qb_s, kb_s, kb_s32, vb_s, vb_s32, qm_s, sza_s, szb_s, szm_s, sg_s) = _in_proj(
        xs, g_in, w_in_b, cos_s, sin_s, keep=n_tok, transposed_v=False)
    per_req = lambda a: a.reshape(dec_b, dec_t, a.shape[-1])

    n_keys = _round_up(a_cache + dec_t, SLAB)
    pad_a = ((0, 0), (0, n_keys - a_cache - dec_t), (0, 0))
    k_cat = jnp.pad(jnp.concatenate([cache_a_k[l].reshape(dec_b, a_cache, BRANCH_W).astype(BF16), per_req(ka_s)], axis=1), pad_a)
    v_cat = jnp.pad(jnp.concatenate([cache_a_v[l].reshape(dec_b, a_cache, BRANCH_W).astype(BF16), per_req(va_s)], axis=1), pad_a)
    ga_s = _chunk_attn_sample(per_req(qa_s), k_cat, v_cat, _sample_bias_table(rel_bias[l], a_cache, dec_t, n_keys),
                              per_req(sza_s))

    n_new = _round_up(dec_t, SLAB)
    pad_b = ((0, 0), (0, n_new - dec_t), (0, 0))
    nbias = jnp.where(jnp.arange(n_new) < dec_t, 0.0, NEG_INF).astype(F32)[None, :]
    gb_s = _diff_attn_sample(per_req(qb_s), cache_b_k[l].reshape(dec_b, past, BRANCH_W),
                             cache_b_v[l].reshape(dec_b, past, BRANCH_W),
                             jnp.pad(per_req(kb_s), pad_b), jnp.pad(per_req(vb_s), pad_b), nbias, per_req(szb_s),
                             lq1, lk1, lq2, lk2, subln[l][None, :], lam_init)
    gm_s = _mem_attn(per_req(qm_s), cache_mem_k[l].reshape(dec_b, n_mem, BRANCH_W),
                     cache_mem_v[l].reshape(dec_b, n_mem, BRANCH_W), per_req(szm_s))
    flat = lambda a: a.reshape(1, n_tok, a.shape[-1])
    y_sample = _merge(flat(ga_s), flat(gb_s), flat(gm_s), sg_s, xs, wa, wb, wm, wo, g_final).reshape(x_sample.shape)

    a_heads, a_dim = cache_a_k.shape[3:]
    bk_heads, bk_dim = cache_b_k.shape[3:]
    bv_heads, bv_dim = cache_b_v.shape[3:]
    m_heads, m_dim = cache_mem_k.shape[3:]
    return (y_prompt, y_sample,
            ka_keep.reshape(1, bsz, keep, a_heads, a_dim), va_keep.reshape(1, bsz, keep, a_heads, a_dim),
            kb_f32.reshape(1, bsz, seq, bk_heads, bk_dim), vb_f32.reshape(1, bsz, seq, bv_heads, bv_dim),
            mk_f32.reshape(1, bsz, n_mem, m_heads, m_dim), mv_f32.reshape(1, bsz, n_mem, m_heads, m_dim),
            ka_s32.reshape(1, dec_b, dec_t, a_heads, a_dim), va_s32.reshape(1, dec_b, dec_t, a_heads, a_dim),
            kb_s32.reshape(1, dec_b, dec_t, bk_heads, bk_dim), vb_s32.reshape(1, dec_b, dec_t, bv_heads, bv_dim))
```

```python
import functools
import math

import jax
import jax.numpy as jnp
from jax import lax
from jax.experimental import pallas as pl
from jax.experimental.pallas import tpu as pltpu

F32 = jnp.float32
BF16 = jnp.bfloat16

CHUNK = 64
CHUNK_SHIFT = 6
A_PAST = 8 * CHUNK
REL_CLIP = 128
BRANCH_W = 512
SLAB = 128
HALF = 64
ROPE_THETA = 10000.0
RMS_EPS = 1e-6
NEG_INF = -1e30

PROJ_ROWS = 512
VT_BLK = 256
KV_STEP = 512
Q_BLK = 128
BAND = A_PAST + Q_BLK
BAND_CONST = A_PAST - REL_CLIP
DQ_BLK = 512
LOG2E = math.log2(math.e)
CHUNK_UNROLL = 32
assert DQ_BLK == KV_STEP

VMEM_LIMIT = 56 * 1024 * 1024


def _dot(a, b):
    return jnp.dot(a, b, preferred_element_type=F32)


def _dot_nt(a, b):
    return lax.dot_general(a, b, (((1,), (1,)), ((), ())), preferred_element_type=F32)


def _silu(z):
    return z * (1.0 / (1.0 + jnp.exp(-z)))


def _sigmoid(z):
    return 1.0 / (1.0 + jnp.exp(-z))


def _split_pair(q):
    lane = lax.broadcasted_iota(jnp.int32, q.shape, 1)
    zero = jnp.zeros_like(q)
    return jnp.concatenate([jnp.where(lane < HALF, q, zero), jnp.where(lane >= HALF, q, zero)], axis=0)


def _diff_lambda(lq1_ref, lk1_ref, lq2_ref, lk2_ref, lam_init):
    e1 = jnp.exp(jnp.sum(lq1_ref[...] * lk1_ref[...], axis=-1, keepdims=True))
    e2 = jnp.exp(jnp.sum(lq2_ref[...] * lk2_ref[...], axis=-1, keepdims=True))
    return e1 - e2 + lam_init


def _in_proj_kernel(x_ref, g_ref, w_ref, cos_ref, sin_ref,
                    qa_ref, ka_ref, va_ref, kaw_ref, vaw_ref,
                    qb_ref, kb_ref, kbf_ref, vb_ref, vbf_ref,
                    qm_ref, sza_ref, szb_ref, szm_ref, sg_ref, *, transposed_v, mem_scale):
    tm = x_ref.shape[0]
    x = x_ref[...]
    h = (x * lax.rsqrt(jnp.mean(x * x, axis=-1, keepdims=True) + RMS_EPS) * g_ref[...]).astype(BF16)

    def proj(idx):
        return _dot(h, w_ref[:, idx * BRANCH_W:(idx + 1) * BRANCH_W])

    cos = cos_ref[...]
    sin = sin_ref[...]
    lane = lax.broadcasted_iota(jnp.int32, (tm, SLAB), 1)
    first_half = (lane & (HALF // 2)) == 0

    def rope_slabs(acc):
        out = []
        for j in range(BRANCH_W // SLAB):
            y = acc[:, j * SLAB:(j + 1) * SLAB]
            partner = jnp.where(first_half, pltpu.roll(y, SLAB - HALF // 2, 1), pltpu.roll(y, HALF // 2, 1))
            out.append(y * cos + partner * sin)
        return out

    qa_ref[...] = (proj(0) * (LOG2E * HALF ** -0.5)).astype(BF16)
    ak = proj(1)
    ka_ref[...] = ak.astype(BF16)
    kaw_ref[...] = ak
    av = proj(2)
    vaw_ref[...] = av
    if transposed_v:
        for p in range(BRANCH_W // SLAB):
            for i in range(tm // SLAB):
                va_ref[p, i] = av[i * SLAB:(i + 1) * SLAB, p * SLAB:(p + 1) * SLAB].T.astype(BF16)
    else:
        va_ref[...] = av.astype(BF16)
    sza_ref[...] = _silu(proj(3)).astype(BF16)

    for j, y in enumerate(rope_slabs(proj(4))):
        qb_ref[:, j * SLAB:(j + 1) * SLAB] = (y * (LOG2E * HALF ** -0.5)).astype(BF16)
    for j, y in enumerate(rope_slabs(proj(5))):
        kbf_ref[:, j * SLAB:(j + 1) * SLAB] = y
        kb_ref[:, j * SLAB:(j + 1) * SLAB] = y.astype(BF16)
    bv = proj(6)
    for hd in range(BRANCH_W // SLAB):
        vbf_ref[:, hd, :] = bv[:, hd * SLAB:(hd + 1) * SLAB]
    if transposed_v:
        for hd in range(BRANCH_W // SLAB):
            for i in range(tm // VT_BLK):
                vb_ref[hd, i] = bv[i * VT_BLK:(i + 1) * VT_BLK, hd * SLAB:(hd + 1) * SLAB].T.astype(BF16)
    else:
        vb_ref[...] = bv.astype(BF16)
    szb_ref[...] = _silu(proj(7)).astype(BF16)

    qm_ref[...] = (proj(8) * mem_scale).astype(BF16)
    szm_ref[...] = _silu(proj(9)).astype(BF16)
    for j in range(sg_ref.shape[1] // BRANCH_W):
        sg_ref[:, j * BRANCH_W:(j + 1) * BRANCH_W] = _sigmoid(proj(10 + j)).astype(BF16)


def _in_proj(x, norm_g, w_bf16, cos_tab, sin_tab, *, keep, transposed_v):
    b, s, d = x.shape
    d_in = w_bf16.shape[1]
    tm = min(PROJ_ROWS, s)
    nt = s // tm
    first_kept = (s - keep) // tm
    n_gate = d_in - 10 * BRANCH_W

    row = lambda bi, j: (bi, j, 0)
    kept = lambda bi, j: (bi, jnp.maximum(j - first_kept, 0), 0)
    act = lambda dt: jax.ShapeDtypeStruct((b, s, BRANCH_W), dt)
    act_spec = pl.BlockSpec((None, tm, BRANCH_W), row)
    win = jax.ShapeDtypeStruct((b, keep, BRANCH_W), F32)
    win_spec = pl.BlockSpec((None, tm, BRANCH_W), kept)
    if transposed_v:
        va = jax.ShapeDtypeStruct((b, BRANCH_W // SLAB, s // SLAB, SLAB, SLAB), BF16)
        va_spec = pl.BlockSpec((None, BRANCH_W // SLAB, tm // SLAB, SLAB, SLAB), lambda bi, j: (bi, 0, j, 0, 0))
        vb = jax.ShapeDtypeStruct((b, BRANCH_W // SLAB, s // VT_BLK, SLAB, VT_BLK), BF16)
        vb_spec = pl.BlockSpec((None, BRANCH_W // SLAB, tm // VT_BLK, SLAB, VT_BLK), lambda bi, j: (bi, 0, j, 0, 0))
    else:
        va, va_spec, vb, vb_spec = act(BF16), act_spec, act(BF16), act_spec

    out_shape = [act(BF16), act(BF16), va, win, win,
                 act(BF16), act(BF16), act(F32), vb, jax.ShapeDtypeStruct((b, s, BRANCH_W // SLAB, SLAB), F32),
                 act(BF16), act(BF16), act(BF16), act(BF16),
                 jax.ShapeDtypeStruct((b, s, n_gate), BF16)]
    out_specs = [act_spec, act_spec, va_spec, win_spec, win_spec,
                 act_spec, act_spec, act_spec, vb_spec,
                 pl.BlockSpec((None, tm, BRANCH_W // SLAB, SLAB), lambda bi, j: (bi, j, 0, 0)),
                 act_spec, act_spec, act_spec, act_spec,
                 pl.BlockSpec((None, tm, n_gate), row)]
    return pl.pallas_call(
        functools.partial(_in_proj_kernel, transposed_v=transposed_v, mem_scale=LOG2E * SLAB ** -0.5),
        grid=(b, nt),
        in_specs=[pl.BlockSpec((None, tm, d), row),
                  pl.BlockSpec((1, d), lambda bi, j: (0, 0)),
                  pl.BlockSpec((d, d_in), lambda bi, j: (0, 0), pipeline_mode=pl.Buffered(1)),
                  pl.BlockSpec((tm, SLAB), lambda bi, j: (j, 0)),
                  pl.BlockSpec((tm, SLAB), lambda bi, j: (j, 0))],
        out_specs=out_specs,
        out_shape=out_shape,
        compiler_params=pltpu.CompilerParams(dimension_semantics=("arbitrary", "arbitrary"),
                                             vmem_limit_bytes=VMEM_LIMIT),
        name="in_proj",
    )(x, norm_g, w_bf16, cos_tab, sin_tab)


def _chunk_attn_prompt_kernel(q_ref, k_ref, vt_ref, bias_ref, sz_ref, o_ref):
    s_len = q_ref.shape[0]

    def q_block(g, carry):
        q0 = pl.multiple_of(g * Q_BLK, Q_BLK)
        q2 = _split_pair(q_ref[pl.ds(q0, Q_BLK), :])
        start = g * Q_BLK - A_PAST
        start_c = jnp.maximum(start, 0)
        off = pl.multiple_of(start_c - start, SLAB)
        start_c = pl.multiple_of(start_c, SLAB)
        kb = k_ref[pl.ds(start_c, BAND), :]
        s = _dot_nt(kb, q2) + bias_ref[pl.ds(off, BAND), :]
        m = jnp.max(s, axis=0, keepdims=True)
        p = jnp.exp2(s - m)
        l = jnp.sum(p, axis=0, keepdims=True)
        pb = p.astype(BF16)
        blk0 = start_c // SLAB
        acc = jnp.zeros((SLAB, 2 * Q_BLK), F32)
        for i in range(BAND // SLAB):
            acc = acc + _dot(vt_ref[blk0 + i], pb[i * SLAB:(i + 1) * SLAB, :])
        acc = acc * (1.0 / l)
        o_t = jnp.concatenate([acc[:HALF, :Q_BLK], acc[HALF:, Q_BLK:]], axis=0)
        o = o_t.T * sz_ref[pl.ds(q0, Q_BLK), :].astype(F32)
        o_ref[pl.ds(q0, Q_BLK), :] = o.astype(BF16)
        return carry

    lax.fori_loop(0, s_len // Q_BLK, q_block, 0, unroll=CHUNK_UNROLL)


def _chunk_attn_prompt(qa, ka, va_t, bias_t, sza):
    b, s, _ = qa.shape
    n_pair = BRANCH_W // SLAB
    slab = pl.BlockSpec((None, s, SLAB), lambda bi, p: (bi, 0, p))
    return pl.pallas_call(
        _chunk_attn_prompt_kernel,
        grid=(b, n_pair),
        in_specs=[slab, slab,
                  pl.BlockSpec((None, None, s // SLAB, SLAB, SLAB), lambda bi, p: (bi, p, 0, 0, 0)),
                  pl.BlockSpec((None, BAND + A_PAST, 2 * Q_BLK), lambda bi, p: (p, 0, 0)),
                  slab],
        out_specs=slab,
        out_shape=jax.ShapeDtypeStruct((b, s, BRANCH_W), BF16),
        compiler_params=pltpu.CompilerParams(dimension_semantics=("arbitrary", "arbitrary"),
                                             vmem_limit_bytes=VMEM_LIMIT),
        name="chunk_attn_prompt",
    )(qa, ka, va_t, bias_t, sza)


def _diff_attn_schedule(n_q):
    steps = [(qi, qi) for qi in range(n_q)]
    steps += [(qi, kv) for kv in range(n_q) for qi in range(kv + 1, n_q)]
    return steps


def _diff_attn_prompt_kernel(sched_ref, q_ref, k_ref, vt_ref, sz_ref, lq1_ref, lk1_ref, lq2_ref, lk2_ref, subln_ref,
                             o_ref, q2_ref, kz_ref, m_ref, l_ref, acc_ref, s_ref, p_ref, *, lam_init):
    n_q = q_ref.shape[0] // DQ_BLK
    n_steps = sched_ref.shape[1]
    sub_blocks = KV_STEP // VT_BLK

    @pl.when((pl.program_id(0) == 0) & (pl.program_id(1) == 0))
    def _():
        k_chunk = jnp.right_shift(lax.broadcasted_iota(jnp.int32, kz_ref.shape[1:], 0), CHUNK_SHIFT)
        chunk_id = lax.broadcasted_iota(jnp.int32, kz_ref.shape[1:], 1)
        kz_ref[0] = jnp.zeros(kz_ref.shape[1:], BF16)
        kz_ref[1] = jnp.where(k_chunk == chunk_id, 1.0, 0.0).astype(BF16)
        q_row = lax.broadcasted_iota(jnp.int32, (2 * DQ_BLK, SLAB), 0)
        q_chunk = jnp.right_shift(q_row & (DQ_BLK - 1), CHUNK_SHIFT)
        hidden = lax.broadcasted_iota(jnp.int32, (2 * DQ_BLK, SLAB), 1) > q_chunk
        for qi in range(n_q):
            q2_ref[qi, :, SLAB:] = jnp.where(hidden, NEG_INF, 0.0).astype(BF16)

    for qi in range(n_q):
        q2_ref[qi, :, :SLAB] = _split_pair(q_ref[qi * DQ_BLK:(qi + 1) * DQ_BLK, :])
    m_ref[...] = jnp.full(m_ref.shape, NEG_INF, F32)
    l_ref[...] = jnp.zeros(l_ref.shape, F32)
    acc_ref[...] = jnp.zeros(acc_ref.shape, F32)
    p_ref[1] = jnp.zeros(p_ref.shape[1:], BF16)

    def scores(t, slot):
        t = jnp.minimum(t, n_steps - 1)
        qi = sched_ref[0, t]
        kv = sched_ref[1, t]
        k0 = pl.multiple_of(kv * KV_STEP, KV_STEP)
        keys = jnp.concatenate([k_ref[pl.ds(k0, KV_STEP), :], kz_ref[jnp.where(qi == kv, 1, 0)]], axis=1)
        s_ref[slot] = _dot_nt(keys, q2_ref[qi])

    def softmax(t, slot):
        qi = sched_ref[0, t]
        s = s_ref[slot]
        m_old = m_ref[qi]
        m = jnp.maximum(m_old, jnp.max(s, axis=0, keepdims=True))
        m_ref[qi] = m
        alpha = jnp.exp2(m_old - m)
        p = jnp.exp2(s - m)
        l_ref[qi] = alpha * l_ref[qi] + jnp.sum(p, axis=0, keepdims=True)
        p_ref[slot] = p.astype(BF16)
        return alpha

    def values(t, slot, alpha):
        t = jnp.maximum(t, 0)
        qi = sched_ref[0, t]
        kv = sched_ref[1, t]
        pv = _dot(vt_ref[sub_blocks * kv], p_ref[slot, :VT_BLK, :])
        for i in range(1, sub_blocks):
            pv = pv + _dot(vt_ref[sub_blocks * kv + i], p_ref[slot, i * VT_BLK:(i + 1) * VT_BLK, :])
        acc_ref[qi] = alpha * acc_ref[qi] + pv

    def two_steps(j, alpha_prev):
        scores(2 * j + 1, 1)
        alpha_even = softmax(2 * j, 0)
        values(2 * j - 1, 1, alpha_prev)
        scores(2 * j + 2, 0)
        alpha_odd = softmax(2 * j + 1, 1)
        values(2 * j, 0, alpha_even)
        return alpha_odd

    scores(0, 0)
    alpha = lax.fori_loop(0, n_steps // 2, two_steps, jnp.ones((1, 2 * DQ_BLK), F32))
    values(n_steps - 1, 1, alpha)

    lam = _diff_lambda(lq1_ref, lk1_ref, lq2_ref, lk2_ref, lam_init)

    def finish(qi, carry):
        q0 = pl.multiple_of(qi * DQ_BLK, DQ_BLK)
        acc = acc_ref[qi] * (1.0 / l_ref[qi])
        o_t = acc[:, :DQ_BLK] - lam * acc[:, DQ_BLK:]
        o_t = o_t * lax.rsqrt(jnp.mean(o_t * o_t, axis=0, keepdims=True) + RMS_EPS)
        o_t = o_t * subln_ref[...] * (1.0 - lam_init)
        o = o_t.T * sz_ref[pl.ds(q0, DQ_BLK), :].astype(F32)
        o_ref[pl.ds(q0, DQ_BLK), :] = o.astype(BF16)
        return carry

    lax.fori_loop(0, n_q, finish, 0)


def _diff_attn_prompt(qb, kb, vb_t, szb, lq1, lk1, lq2, lk2, subln_col, lam_init):
    b, s, _ = qb.shape
    n_head = BRANCH_W // SLAB
    n_q = s // DQ_BLK
    steps = _diff_attn_schedule(n_q)
    assert len(steps) % 2 == 0, "the pipeline advances two steps per iteration"
    assert KV_STEP // CHUNK <= SLAB, "chunk ids of a diagonal block must fit the spare contraction lanes"
    sched = jnp.asarray([[qi for qi, _ in steps], [kv for _, kv in steps]], jnp.int32)
    slab = pl.BlockSpec((None, s, SLAB), lambda bi, hd: (bi, 0, hd))
    small = lambda shape: pl.BlockSpec(shape, lambda bi, hd: (0, 0))
    return pl.pallas_call(
        functools.partial(_diff_attn_prompt_kernel, lam_init=lam_init),
        grid=(b, n_head),
        in_specs=[pl.BlockSpec(memory_space=pltpu.SMEM), slab, slab,
                  pl.BlockSpec((None, None, s // VT_BLK, SLAB, VT_BLK), lambda bi, hd: (bi, hd, 0, 0, 0)),
                  slab,
                  small(lq1.shape), small(lk1.shape), small(lq2.shape), small(lk2.shape), small(subln_col.shape)],
        out_specs=slab,
        out_shape=jax.ShapeDtypeStruct((b, s, BRANCH_W), BF16),
        scratch_shapes=[pltpu.VMEM((n_q, 2 * DQ_BLK, 2 * SLAB), BF16),
                        pltpu.VMEM((2, KV_STEP, SLAB), BF16),
                        pltpu.VMEM((n_q, 1, 2 * DQ_BLK), F32),
                        pltpu.VMEM((n_q, 1, 2 * DQ_BLK), F32),
                        pltpu.VMEM((n_q, SLAB, 2 * DQ_BLK), F32),
                        pltpu.VMEM((2, KV_STEP, 2 * DQ_BLK), F32),
                        pltpu.VMEM((2, KV_STEP, 2 * DQ_BLK), BF16)],
        compiler_params=pltpu.CompilerParams(dimension_semantics=("arbitrary", "arbitrary"),
                                             vmem_limit_bytes=VMEM_LIMIT),
        name="diff_attn_prompt",
    )(sched, qb, kb, vb_t, szb, lq1, lk1, lq2, lk2, subln_col)


def _mem_kv_kernel(mem_ref, g_ref, w_ref, kf_ref, vf_ref, kb_ref, vb_ref):
    x = mem_ref[...]
    h = (x * lax.rsqrt(jnp.mean(x * x, axis=-1, keepdims=True) + RMS_EPS) * g_ref[...]).astype(BF16)
    k = _dot(h, w_ref[:, :BRANCH_W])
    v = _dot(h, w_ref[:, BRANCH_W:])
    kf_ref[...] = k
    vf_ref[...] = v
    kb_ref[...] = k.astype(BF16)
    vb_ref[...] = v.astype(BF16)


def _mem_kv(mem, norm_g, w_bf16):
    b, n, d = mem.shape
    blk = lambda width: pl.BlockSpec((None, n, width), lambda bi: (bi, 0, 0))
    out = lambda dt: jax.ShapeDtypeStruct((b, n, BRANCH_W), dt)
    return pl.pallas_call(
        _mem_kv_kernel,
        grid=(b,),
        in_specs=[blk(d), pl.BlockSpec((1, d), lambda bi: (0, 0)),
                  pl.BlockSpec((d, 2 * BRANCH_W), lambda bi: (0, 0))],
        out_specs=[blk(BRANCH_W)] * 4,
        out_shape=[out(F32), out(F32), out(BF16), out(BF16)],
        compiler_params=pltpu.CompilerParams(dimension_semantics=("arbitrary",)),
        name="mem_kv",
    )(mem, norm_g, w_bf16)


def _mem_attn_kernel(q_ref, k_ref, v_ref, sz_ref, o_ref):
    for hd in range(BRANCH_W // SLAB):
        cols = slice(hd * SLAB, (hd + 1) * SLAB)
        s = _dot_nt(q_ref[:, cols], k_ref[:, cols].astype(BF16))
        m = jnp.max(s, axis=-1, keepdims=True)
        p = jnp.exp2(s - m)
        l = jnp.sum(p, axis=-1, keepdims=True)
        o = _dot(p.astype(BF16), v_ref[:, cols].astype(BF16)) * (1.0 / l)
        o_ref[:, cols] = (o * sz_ref[:, cols].astype(F32)).astype(BF16)


def _mem_attn(qm, mk, mv, szm):
    b, s, _ = qm.shape
    n = mk.shape[1]
    tq = min(PROJ_ROWS, s)
    row = pl.BlockSpec((None, tq, BRANCH_W), lambda bi, j: (bi, j, 0))
    mem = pl.BlockSpec((None, n, BRANCH_W), lambda bi, j: (bi, 0, 0))
    return pl.pallas_call(
        _mem_attn_kernel,
        grid=(b, s // tq),
        in_specs=[row, mem, mem, row],
        out_specs=row,
        out_shape=jax.ShapeDtypeStruct((b, s, BRANCH_W), BF16),
        compiler_params=pltpu.CompilerParams(dimension_semantics=("arbitrary", "arbitrary")),
        name="mem_attn",
    )(qm, mk, mv, szm)


def _merge_kernel(ga_ref, gb_ref, gm_ref, sg_ref, x_ref, wa_ref, wb_ref, wm_ref, wo_ref, nf_ref, y_ref):
    d = x_ref.shape[1]
    mixed = sg_ref[:, :d].astype(F32) * _dot(ga_ref[...], wa_ref[...])
    mixed = mixed + sg_ref[:, d:2 * d].astype(F32) * _dot(gb_ref[...], wb_ref[...])
    mixed = mixed + sg_ref[:, 2 * d:].astype(F32) * _dot(gm_ref[...], wm_ref[...])
    hsum = x_ref[...] + _dot(mixed.astype(BF16), wo_ref[...])
    y_ref[...] = hsum * lax.rsqrt(jnp.mean(hsum * hsum, axis=-1, keepdims=True) + RMS_EPS) * nf_ref[...]


def _merge(ga, gb, gm, sg, x, wa, wb, wm, wo, norm_final):
    b, s, d = x.shape
    tm = min(PROJ_ROWS, s)
    row = lambda width: pl.BlockSpec((None, tm, width), lambda bi, j: (bi, j, 0))
    full = lambda a: pl.BlockSpec(a.shape, lambda bi, j: (0, 0))
    return pl.pallas_call(
        _merge_kernel,
        grid=(b, s // tm),
        in_specs=[row(BRANCH_W), row(BRANCH_W), row(BRANCH_W), row(sg.shape[2]), row(d),
                  full(wa), full(wb), full(wm), full(wo), full(norm_final)],
        out_specs=row(d),
        out_shape=jax.ShapeDtypeStruct((b, s, d), F32),
        compiler_params=pltpu.CompilerParams(dimension_semantics=("arbitrary", "arbitrary"),
                                             vmem_limit_bytes=VMEM_LIMIT),
        name="merge",
    )(ga, gb, gm, sg, x, wa, wb, wm, wo, norm_final)


def _chunk_attn_sample_kernel(q_ref, k_ref, v_ref, bias_ref, sz_ref, o_ref):
    t = q_ref.shape[0]
    lane = lax.broadcasted_iota(jnp.int32, (t, SLAB), 1)
    for p in range(BRANCH_W // SLAB):
        cols = slice(p * SLAB, (p + 1) * SLAB)
        q2 = _split_pair(q_ref[:, cols])
        s = _dot_nt(q2, k_ref[:, cols]) + bias_ref[p]
        m = jnp.max(s, axis=-1, keepdims=True)
        pr = jnp.exp2(s - m)
        l = jnp.sum(pr, axis=-1, keepdims=True)
        o = _dot(pr.astype(BF16), v_ref[:, cols]) * (1.0 / l)
        o = jnp.where(lane < HALF, o[:t], o[t:])
        o_ref[:, cols] = (o * sz_ref[:, cols].astype(F32)).astype(BF16)


def _chunk_attn_sample(qa, k_cat, v_cat, bias, sza):
    b, t, _ = qa.shape
    n_keys = k_cat.shape[1]
    row = pl.BlockSpec((None, t, BRANCH_W), lambda bi: (bi, 0, 0))
    keys = pl.BlockSpec((None, n_keys, BRANCH_W), lambda bi: (bi, 0, 0))
    return pl.pallas_call(
        _chunk_attn_sample_kernel,
        grid=(b,),
        in_specs=[row, keys, keys, pl.BlockSpec(bias.shape, lambda bi: (0, 0, 0)), row],
        out_specs=row,
        out_shape=jax.ShapeDtypeStruct((b, t, BRANCH_W), BF16),
        compiler_params=pltpu.CompilerParams(dimension_semantics=("arbitrary",)),
        name="chunk_attn_sample",
    )(qa, k_cat, v_cat, bias, sza)


def _diff_attn_sample_kernel(q_ref, kt_ref, vc_ref, kn_ref, vn_ref, nbias_ref, sz_ref,
                             lq1_ref, lk1_ref, lq2_ref, lk2_ref, subln_ref, o_ref, *, lam_init):
    t = q_ref.shape[0]
    past = kt_ref.shape[2]
    lam = _diff_lambda(lq1_ref, lk1_ref, lq2_ref, lk2_ref, lam_init)
    for hd in range(BRANCH_W // SLAB):
        cols = slice(hd * SLAB, (hd + 1) * SLAB)
        q2 = _split_pair(q_ref[:, cols])
        kt = kt_ref[2 * hd:2 * hd + 2].reshape(SLAB, past).astype(BF16)
        s_c = _dot(q2, kt)
        s_n = _dot_nt(q2, kn_ref[:, cols]) + nbias_ref[...]
        m = jnp.maximum(jnp.max(s_c, axis=-1, keepdims=True), jnp.max(s_n, axis=-1, keepdims=True))
        p_c = jnp.exp2(s_c - m)
        p_n = jnp.exp2(s_n - m)
        l = jnp.sum(p_c, axis=-1, keepdims=True) + jnp.sum(p_n, axis=-1, keepdims=True)
        o = _dot(p_c.astype(BF16), vc_ref[:, cols].astype(BF16)) + _dot(p_n.astype(BF16), vn_ref[:, cols])
        o = o * (1.0 / l)
        od = o[:t] - lam * o[t:]
        od = od * lax.rsqrt(jnp.mean(od * od, axis=-1, keepdims=True) + RMS_EPS) * subln_ref[...] * (1.0 - lam_init)
        o_ref[:, cols] = (od * sz_ref[:, cols].astype(F32)).astype(BF16)


def _diff_attn_sample(qb, cache_kt, cache_v, k_new, v_new, nbias, szb, lq1, lk1, lq2, lk2, subln_row, lam_init):
    b, t, _ = qb.shape
    n_new = k_new.shape[1]
    tok = pl.BlockSpec((None, t, BRANCH_W), lambda bi: (bi, 0, 0))
    new = pl.BlockSpec((None, n_new, BRANCH_W), lambda bi: (bi, 0, 0))
    whole = lambda a: pl.BlockSpec((None,) + a.shape[1:], lambda bi: (bi,) + (0,) * (a.ndim - 1))
    small = lambda a: pl.BlockSpec(a.shape, lambda bi: (0, 0))
    return pl.pallas_call(
        functools.partial(_diff_attn_sample_kernel, lam_init=lam_init),
        grid=(b,),
        in_specs=[tok, whole(cache_kt), whole(cache_v), new, new, small(nbias), tok,
                  small(lq1), small(lk1), small(lq2), small(lk2), small(subln_row)],
        out_specs=tok,
        out_shape=jax.ShapeDtypeStruct((b, t, BRANCH_W), BF16),
        compiler_params=pltpu.CompilerParams(dimension_semantics=("arbitrary",), vmem_limit_bytes=VMEM_LIMIT),
        name="diff_attn_sample",
    )(qb, cache_kt, cache_v, k_new, v_new, nbias, szb, lq1, lk1, lq2, lk2, subln_row)


def _rope_tables(pos):
    inv = 1.0 / (ROPE_THETA ** (jnp.arange(0, HALF, 2, dtype=F32) / HALF))
    ang = pos.astype(F32)[:, None] * inv[None, :]
    cos = jnp.cos(ang)
    sin = jnp.sin(ang)
    return jnp.tile(cos, (1, SLAB // (HALF // 2))), jnp.tile(jnp.concatenate([-sin, sin], axis=1), (1, SLAB // HALF))


def _bias_lookup(rb_ref, pair, dist, head_in_pair):
    idx = jnp.clip(dist, -REL_CLIP, REL_CLIP) + REL_CLIP

    def entry(k, acc):
        v = jnp.where(head_in_pair == 1, rb_ref[2 * pair + 1, k], rb_ref[2 * pair, k])
        return jnp.where(idx == k, v, acc)

    return LOG2E * lax.fori_loop(0, 2 * REL_CLIP + 1, entry, jnp.zeros(dist.shape, F32))


def _prompt_bias_kernel(rb_ref, o_ref):
    pair = pl.program_id(0)

    def rows(u0, n):
        u = u0 + lax.broadcasted_iota(jnp.int32, (n, 2 * Q_BLK), 0)
        c = lax.broadcasted_iota(jnp.int32, (n, 2 * Q_BLK), 1)
        ql = c & (Q_BLK - 1)
        visible = ((ql < CHUNK) & (u < A_PAST + CHUNK)) | ((ql >= CHUNK) & (u >= CHUNK))
        return ql + A_PAST - u, jnp.where(c >= Q_BLK, 1, 0), visible

    dist, head_in_pair, visible = rows(BAND_CONST, BAND - BAND_CONST)
    o_ref[BAND_CONST:BAND, :] = jnp.where(visible, _bias_lookup(rb_ref, pair, dist, head_in_pair), NEG_INF)
    _, head_in_pair, visible = rows(0, BAND_CONST)
    far = LOG2E * jnp.where(head_in_pair == 1, rb_ref[2 * pair + 1, 2 * REL_CLIP], rb_ref[2 * pair, 2 * REL_CLIP])
    o_ref[:BAND_CONST, :] = jnp.where(visible, far, NEG_INF)
    o_ref[BAND:, :] = jnp.full((o_ref.shape[0] - BAND, 2 * Q_BLK), NEG_INF, F32)


def _prompt_bias_table(rel_bias):
    n_pair = rel_bias.shape[0] // 2
    rows = BAND + A_PAST
    return pl.pallas_call(
        _prompt_bias_kernel,
        grid=(n_pair,),
        in_specs=[pl.BlockSpec(memory_space=pltpu.SMEM)],
        out_specs=pl.BlockSpec((None, rows, 2 * Q_BLK), lambda p: (p, 0, 0)),
        out_shape=jax.ShapeDtypeStruct((n_pair, rows, 2 * Q_BLK), F32),
        compiler_params=pltpu.CompilerParams(dimension_semantics=("arbitrary",)),
        name="prompt_bias",
    )(rel_bias)


def _sample_bias_kernel(rb_ref, o_ref, *, p_len, t):
    pair = pl.program_id(0)
    r = lax.broadcasted_iota(jnp.int32, o_ref.shape, 0)
    j = lax.broadcasted_iota(jnp.int32, o_ref.shape, 1)
    head_in_pair = jnp.where(r >= t, 1, 0)
    dist = p_len + r - t * head_in_pair - j
    o_ref[...] = jnp.where(j < p_len + t, _bias_lookup(rb_ref, pair, dist, head_in_pair), NEG_INF)


def _sample_bias_table(rel_bias, p_len, t, n_keys):
    n_pair = rel_bias.shape[0] // 2
    return pl.pallas_call(
        functools.partial(_sample_bias_kernel, p_len=p_len, t=t),
        grid=(n_pair,),
        in_specs=[pl.BlockSpec(memory_space=pltpu.SMEM)],
        out_specs=pl.BlockSpec((None, 2 * t, n_keys), lambda p: (p, 0, 0)),
        out_shape=jax.ShapeDtypeStruct((n_pair, 2 * t, n_keys), F32),
        compiler_params=pltpu.CompilerParams(dimension_semantics=("arbitrary",)),
        name="sample_bias",
    )(rel_bias)


def _round_up(n, m):
    return (n + m - 1) // m * m


def kernel(x_prompt, x_sample, cache_a_k, cache_a_v, cache_b_k, cache_b_v, cache_mem_k, cache_mem_v, mem_prompt, norm_in, w_in, rel_bias, lambda_q1, lambda_k1, lambda_q2, lambda_k2, subln, norm_mem, w_mem_kv, w_branch_a, w_branch_b, w_branch_m, w_out, norm_final):
    depth = w_in.shape[0]
    assert depth == 1, "kernels are written for the single-layer step"
    bsz, seq, d_model = x_prompt.shape
    dec_b, dec_t, _ = x_sample.shape
    past = cache_b_k.shape[2]
    a_cache = cache_a_k.shape[2]
    n_mem = mem_prompt.shape[1]
    keep = min(A_PAST, seq)
    lam_init = 0.8 - 0.6 * math.exp(-0.3 * 0)
    l = 0

    w_in_b = w_in[l].astype(BF16)
    w_mem_b = w_mem_kv[l].astype(BF16)
    wa, wb, wm, wo = (w[l].astype(BF16) for w in (w_branch_a, w_branch_b, w_branch_m, w_out))
    g_in = norm_in[l][None, :]
    g_mem = norm_mem[l][None, :]
    g_final = norm_final[None, :]
    lq1, lk1, lq2, lk2 = (v[l][None, :] for v in (lambda_q1, lambda_k1, lambda_q2, lambda_k2))

    cos_p, sin_p = _rope_tables(jnp.arange(seq))
    (qa, ka, va_t, ka_keep, va_keep, qb, kb, kb_f32, vb_t, vb_f32, qm, sza, szb, szm, sg) = _in_proj(
        x_prompt, g_in, w_in_b, cos_p, sin_p, keep=keep, transposed_v=True)
    ga = _chunk_attn_prompt(qa, ka, va_t, _prompt_bias_table(rel_bias[l]), sza)
    gb = _diff_attn_prompt(qb, kb, vb_t, szb, lq1, lk1, lq2, lk2, subln[l][:, None], lam_init)
    mk_f32, mv_f32, mk, mv = _mem_kv(mem_prompt, g_mem, w_mem_b)
    gm = _mem_attn(qm, mk, mv, szm)
    y_prompt = _merge(ga, gb, gm, sg, x_prompt, wa, wb, wm, wo, g_final)

    n_tok = dec_b * dec_t
    pos_s = past + jnp.tile(jnp.arange(dec_t), dec_b)
    cos_s, sin_s = _rope_tables(pos_s)
    xs = x_sample.reshape(1, n_tok, d_model)
    (qa_s, ka_s, va_s, ka_s32, va_s32, qb_s, kb_s, kb_s32, vb_s, vb_s32, qm_s, sza_s, szb_s, szm_s, sg_s) = _in_proj(
        xs, g_in, w_in_b, cos_s, sin_s, keep=n_tok, transposed_v=False)
    per_req = lambda a: a.reshape(dec_b, dec_t, a.shape[-1])

    n_keys = _round_up(a_cache + dec_t, SLAB)
    pad_a = ((0, 0), (0, n_keys - a_cache - dec_t), (0, 0))
    k_cat = jnp.pad(jnp.concatenate([cache_a_k[l].reshape(dec_b, a_cache, BRANCH_W).astype(BF16), per_req(ka_s)], axis=1), pad_a)
    v_cat = jnp.pad(jnp.concatenate([cache_a_v[l].reshape(dec_b, a_cache, BRANCH_W).astype(BF16), per_req(va_s)], axis=1), pad_a)
    ga_s = _chunk_attn_sample(per_req(qa_s), k_cat, v_cat, _sample_bias_table(rel_bias[l], a_cache, dec_t, n_keys),
                              per_req(sza_s))

    n_new = _round_up(dec_t, SLAB)
    pad_b = ((0, 0), (0, n_new - dec_t), (0, 0))
    nbias = jnp.where(jnp.arange(n_new) < dec_t, 0.0, NEG_INF).astype(F32)[None, :]
    gb_s = _diff_attn_sample(per_req(qb_s), jnp.transpose(cache_b_k[l], (0, 2, 3, 1)),
                             cache_b_v[l].reshape(dec_b, past, BRANCH_W),
                             jnp.pad(per_req(kb_s), pad_b), jnp.pad(per_req(vb_s), pad_b), nbias, per_req(szb_s),
                             lq1, lk1, lq2, lk2, subln[l][None, :], lam_init)
    gm_s = _mem_attn(per_req(qm_s), cache_mem_k[l].reshape(dec_b, n_mem, BRANCH_W),
                     cache_mem_v[l].reshape(dec_b, n_mem, BRANCH_W), per_req(szm_s))
    flat = lambda a: a.reshape(1, n_tok, a.shape[-1])
    y_sample = _merge(flat(ga_s), flat(gb_s), flat(gm_s), sg_s, xs, wa, wb, wm, wo, g_final).reshape(x_sample.shape)

    a_heads, a_dim = cache_a_k.shape[3:]
    bk_heads, bk_dim = cache_b_k.shape[3:]
    bv_heads, bv_dim = cache_b_v.shape[3:]
    m_heads, m_dim = cache_mem_k.shape[3:]
    return (y_prompt, y_sample,
            ka_keep.reshape(1, bsz, keep, a_heads, a_dim), va_keep.reshape(1, bsz, keep, a_heads, a_dim),
            kb_f32.reshape(1, bsz, seq, bk_heads, bk_dim), vb_f32.reshape(1, bsz, seq, bv_heads, bv_dim),
            mk_f32.reshape(1, bsz, n_mem, m_heads, m_dim), mv_f32.reshape(1, bsz, n_mem, m_heads, m_dim),
            ka_s32.reshape(1, dec_b, dec_t, a_heads, a_dim), va_s32.reshape(1, dec_b, dec_t, a_heads, a_dim),
            kb_s32.reshape(1, dec_b, dec_t, bk_heads, bk_dim), vb_s32.reshape(1, dec_b, dec_t, bv_heads, bv_dim))
```

```python
import functools
import math

import jax
import jax.numpy as jnp
from jax import lax
from jax.experimental import pallas as pl
from jax.experimental.pallas import tpu as pltpu

F32 = jnp.float32
BF16 = jnp.bfloat16

CHUNK = 64
CHUNK_SHIFT = 6
A_PAST = 8 * CHUNK
REL_CLIP = 128
BRANCH_W = 512
SLAB = 128
HALF = 64
ROPE_THETA = 10000.0
RMS_EPS = 1e-6
NEG_INF = -1e30

PROJ_ROWS = 512
VT_BLK = 256
VT_ROWS = SLAB + 16
KV_STEP = 512
Q_BLK = 128
BAND = A_PAST + Q_BLK
BAND_CONST = A_PAST - REL_CLIP
DQ_BLK = 512
LOG2E = math.log2(math.e)
PIPE_SLOTS = 3
COL_GROUP = 256
CHUNK_UNROLL = 32
assert DQ_BLK == KV_STEP

VMEM_LIMIT = 56 * 1024 * 1024


def _dot(a, b):
    return jnp.dot(a, b, preferred_element_type=F32)


def _dot_nt(a, b):
    return lax.dot_general(a, b, (((1,), (1,)), ((), ())), preferred_element_type=F32)


def _silu(z):
    return z * (1.0 / (1.0 + jnp.exp(-z)))


def _sigmoid(z):
    return 1.0 / (1.0 + jnp.exp(-z))


def _split_pair(q):
    lane = lax.broadcasted_iota(jnp.int32, q.shape, 1)
    zero = jnp.zeros_like(q)
    return jnp.concatenate([jnp.where(lane < HALF, q, zero), jnp.where(lane >= HALF, q, zero)], axis=0)


def _diff_lambda(lq1_ref, lk1_ref, lq2_ref, lk2_ref, lam_init):
    e1 = jnp.exp(jnp.sum(lq1_ref[...] * lk1_ref[...], axis=-1, keepdims=True))
    e2 = jnp.exp(jnp.sum(lq2_ref[...] * lk2_ref[...], axis=-1, keepdims=True))
    return e1 - e2 + lam_init


def _in_proj_kernel(x_ref, g_ref, w_ref, cos_ref, sin_ref,
                    qa_ref, ka_ref, va_ref, kaw_ref, vaw_ref,
                    qb_ref, kb_ref, kbf_ref, vb_ref, vbf_ref,
                    qm_ref, sza_ref, szb_ref, szm_ref, sg_ref, *, transposed_v, mem_scale):
    tm = x_ref.shape[0]
    x = x_ref[...]
    h = (x * lax.rsqrt(jnp.mean(x * x, axis=-1, keepdims=True) + RMS_EPS) * g_ref[...]).astype(BF16)

    def proj(idx):
        return _dot(h, w_ref[:, idx * BRANCH_W:(idx + 1) * BRANCH_W])

    cos = cos_ref[...]
    sin = sin_ref[...]
    lane = lax.broadcasted_iota(jnp.int32, (tm, SLAB), 1)
    first_half = (lane & (HALF // 2)) == 0

    def rope_slabs(acc):
        out = []
        for j in range(BRANCH_W // SLAB):
            y = acc[:, j * SLAB:(j + 1) * SLAB]
            partner = jnp.where(first_half, pltpu.roll(y, SLAB - HALF // 2, 1), pltpu.roll(y, HALF // 2, 1))
            out.append(y * cos + partner * sin)
        return out

    qa_ref[...] = (proj(0) * (LOG2E * HALF ** -0.5)).astype(BF16)
    ak = proj(1)
    ka_ref[...] = ak.astype(BF16)
    kaw_ref[...] = ak
    av = proj(2)
    vaw_ref[...] = av
    if transposed_v:
        for p in range(BRANCH_W // SLAB):
            for i in range(tm // SLAB):
                va_ref[p, i] = av[i * SLAB:(i + 1) * SLAB, p * SLAB:(p + 1) * SLAB].T.astype(BF16)
    else:
        va_ref[...] = av.astype(BF16)
    sza_ref[...] = _silu(proj(3)).astype(BF16)

    for j, y in enumerate(rope_slabs(proj(4))):
        qb_ref[:, j * SLAB:(j + 1) * SLAB] = (y * (LOG2E * HALF ** -0.5)).astype(BF16)
    for j, y in enumerate(rope_slabs(proj(5))):
        kbf_ref[:, j * SLAB:(j + 1) * SLAB] = y
        kb_ref[:, j * SLAB:(j + 1) * SLAB] = y.astype(BF16)
    bv = proj(6)
    for hd in range(BRANCH_W // SLAB):
        vbf_ref[:, hd, :] = bv[:, hd * SLAB:(hd + 1) * SLAB]
    if transposed_v:
        for hd in range(BRANCH_W // SLAB):
            for i in range(tm // VT_BLK):
                vb_ref[hd, i, :SLAB, :] = bv[i * VT_BLK:(i + 1) * VT_BLK, hd * SLAB:(hd + 1) * SLAB].T.astype(BF16)
                vb_ref[hd, i, SLAB:, :] = jnp.ones((VT_ROWS - SLAB, VT_BLK), BF16)
    else:
        vb_ref[...] = bv.astype(BF16)
    szb_ref[...] = _silu(proj(7)).astype(BF16)

    qm_ref[...] = (proj(8) * mem_scale).astype(BF16)
    szm_ref[...] = _silu(proj(9)).astype(BF16)
    for j in range(sg_ref.shape[1] // BRANCH_W):
        sg_ref[:, j * BRANCH_W:(j + 1) * BRANCH_W] = _sigmoid(proj(10 + j)).astype(BF16)


def _in_proj(x, norm_g, w_bf16, cos_tab, sin_tab, *, keep, transposed_v):
    b, s, d = x.shape
    d_in = w_bf16.shape[1]
    tm = min(PROJ_ROWS, s)
    nt = s // tm
    first_kept = (s - keep) // tm
    n_gate = d_in - 10 * BRANCH_W

    row = lambda bi, j: (bi, j, 0)
    kept = lambda bi, j: (bi, jnp.maximum(j - first_kept, 0), 0)
    act = lambda dt: jax.ShapeDtypeStruct((b, s, BRANCH_W), dt)
    act_spec = pl.BlockSpec((None, tm, BRANCH_W), row)
    win = jax.ShapeDtypeStruct((b, keep, BRANCH_W), F32)
    win_spec = pl.BlockSpec((None, tm, BRANCH_W), kept)
    if transposed_v:
        va = jax.ShapeDtypeStruct((b, BRANCH_W // SLAB, s // SLAB, SLAB, SLAB), BF16)
        va_spec = pl.BlockSpec((None, BRANCH_W // SLAB, tm // SLAB, SLAB, SLAB), lambda bi, j: (bi, 0, j, 0, 0))
        vb = jax.ShapeDtypeStruct((b, BRANCH_W // SLAB, s // VT_BLK, VT_ROWS, VT_BLK), BF16)
        vb_spec = pl.BlockSpec((None, BRANCH_W // SLAB, tm // VT_BLK, VT_ROWS, VT_BLK), lambda bi, j: (bi, 0, j, 0, 0))
    else:
        va, va_spec, vb, vb_spec = act(BF16), act_spec, act(BF16), act_spec

    out_shape = [act(BF16), act(BF16), va, win, win,
                 act(BF16), act(BF16), act(F32), vb, jax.ShapeDtypeStruct((b, s, BRANCH_W // SLAB, SLAB), F32),
                 act(BF16), act(BF16), act(BF16), act(BF16),
                 jax.ShapeDtypeStruct((b, s, n_gate), BF16)]
    out_specs = [act_spec, act_spec, va_spec, win_spec, win_spec,
                 act_spec, act_spec, act_spec, vb_spec,
                 pl.BlockSpec((None, tm, BRANCH_W // SLAB, SLAB), lambda bi, j: (bi, j, 0, 0)),
                 act_spec, act_spec, act_spec, act_spec,
                 pl.BlockSpec((None, tm, n_gate), row)]
    return pl.pallas_call(
        functools.partial(_in_proj_kernel, transposed_v=transposed_v, mem_scale=LOG2E * SLAB ** -0.5),
        grid=(b, nt),
        in_specs=[pl.BlockSpec((None, tm, d), row),
                  pl.BlockSpec((1, d), lambda bi, j: (0, 0)),
                  pl.BlockSpec((d, d_in), lambda bi, j: (0, 0), pipeline_mode=pl.Buffered(1)),
                  pl.BlockSpec((tm, SLAB), lambda bi, j: (j, 0)),
                  pl.BlockSpec((tm, SLAB), lambda bi, j: (j, 0))],
        out_specs=out_specs,
        out_shape=out_shape,
        compiler_params=pltpu.CompilerParams(dimension_semantics=("arbitrary", "arbitrary"),
                                             vmem_limit_bytes=VMEM_LIMIT),
        name="in_proj",
    )(x, norm_g, w_bf16, cos_tab, sin_tab)


def _chunk_attn_prompt_kernel(q_ref, k_ref, vt_ref, bias_ref, sz_ref, o_ref):
    s_len = q_ref.shape[0]

    def q_block(g, carry):
        q0 = pl.multiple_of(g * Q_BLK, Q_BLK)
        q2 = _split_pair(q_ref[pl.ds(q0, Q_BLK), :])
        start = g * Q_BLK - A_PAST
        start_c = jnp.maximum(start, 0)
        off = pl.multiple_of(start_c - start, SLAB)
        start_c = pl.multiple_of(start_c, SLAB)
        kb = k_ref[pl.ds(start_c, BAND), :]
        s = _dot_nt(kb, q2) + bias_ref[pl.ds(off, BAND), :]
        m = jnp.max(s, axis=0, keepdims=True)
        p = jnp.exp2(s - m)
        l = jnp.sum(p, axis=0, keepdims=True)
        pb = p.astype(BF16)
        blk0 = start_c // SLAB
        acc = jnp.zeros((SLAB, 2 * Q_BLK), F32)
        for i in range(BAND // SLAB):
            acc = acc + _dot(vt_ref[blk0 + i], pb[i * SLAB:(i + 1) * SLAB, :])
        acc = acc * (1.0 / l)
        o_t = jnp.concatenate([acc[:HALF, :Q_BLK], acc[HALF:, Q_BLK:]], axis=0)
        o = o_t.T * sz_ref[pl.ds(q0, Q_BLK), :].astype(F32)
        o_ref[pl.ds(q0, Q_BLK), :] = o.astype(BF16)
        return carry

    lax.fori_loop(0, s_len // Q_BLK, q_block, 0, unroll=CHUNK_UNROLL)


def _chunk_attn_prompt(qa, ka, va_t, bias_t, sza):
    b, s, _ = qa.shape
    n_pair = BRANCH_W // SLAB
    slab = pl.BlockSpec((None, s, SLAB), lambda bi, p: (bi, 0, p))
    return pl.pallas_call(
        _chunk_attn_prompt_kernel,
        grid=(b, n_pair),
        in_specs=[slab, slab,
                  pl.BlockSpec((None, None, s // SLAB, SLAB, SLAB), lambda bi, p: (bi, p, 0, 0, 0)),
                  pl.BlockSpec((None, BAND + A_PAST, 2 * Q_BLK), lambda bi, p: (p, 0, 0)),
                  slab],
        out_specs=slab,
        out_shape=jax.ShapeDtypeStruct((b, s, BRANCH_W), BF16),
        compiler_params=pltpu.CompilerParams(dimension_semantics=("arbitrary", "arbitrary"),
                                             vmem_limit_bytes=VMEM_LIMIT),
        name="chunk_attn_prompt",
    )(qa, ka, va_t, bias_t, sza)


def _diff_attn_schedule(n_q):
    steps = [(qi, qi) for qi in range(n_q)]
    steps += [(qi, kv) for kv in range(n_q) for qi in range(kv + 1, n_q)]
    return steps


def _diff_attn_prompt_kernel(sched_ref, q_ref, k_ref, vt_ref, sz_ref, lq1_ref, lk1_ref, lq2_ref, lk2_ref, subln_ref,
                             o_ref, q2_ref, kz_ref, m_ref, acc_ref, s_ref, p_ref, *, lam_init):
    n_q = q_ref.shape[0] // DQ_BLK
    n_steps = sched_ref.shape[1]
    sub_blocks = KV_STEP // VT_BLK

    @pl.when((pl.program_id(0) == 0) & (pl.program_id(1) == 0))
    def _():
        k_chunk = jnp.right_shift(lax.broadcasted_iota(jnp.int32, kz_ref.shape[1:], 0), CHUNK_SHIFT)
        chunk_id = lax.broadcasted_iota(jnp.int32, kz_ref.shape[1:], 1)
        kz_ref[0] = jnp.zeros(kz_ref.shape[1:], BF16)
        kz_ref[1] = jnp.where(k_chunk == chunk_id, 1.0, 0.0).astype(BF16)
        q_row = lax.broadcasted_iota(jnp.int32, (2 * DQ_BLK, SLAB), 0)
        q_chunk = jnp.right_shift(q_row & (DQ_BLK - 1), CHUNK_SHIFT)
        hidden = lax.broadcasted_iota(jnp.int32, (2 * DQ_BLK, SLAB), 1) > q_chunk
        for qi in range(n_q):
            q2_ref[qi, :, SLAB:] = jnp.where(hidden, NEG_INF, 0.0).astype(BF16)

    for qi in range(n_q):
        q2_ref[qi, :, :SLAB] = _split_pair(q_ref[qi * DQ_BLK:(qi + 1) * DQ_BLK, :])
    m_ref[...] = jnp.full(m_ref.shape, NEG_INF, F32)
    acc_ref[...] = jnp.zeros(acc_ref.shape, F32)
    p_ref[PIPE_SLOTS - 1] = jnp.zeros(p_ref.shape[1:], BF16)

    n_groups = 2 * DQ_BLK // COL_GROUP

    def cols(g):
        return slice(g * COL_GROUP, (g + 1) * COL_GROUP)

    def scores(t, slot, g):
        t = jnp.minimum(t, n_steps - 1)
        qi = sched_ref[0, t]
        kv = sched_ref[1, t]
        k0 = pl.multiple_of(kv * KV_STEP, KV_STEP)
        keys = jnp.concatenate([k_ref[pl.ds(k0, KV_STEP), :], kz_ref[jnp.where(qi == kv, 1, 0)]], axis=1)
        s_ref[slot, :, cols(g)] = _dot_nt(keys, q2_ref[qi, cols(g), :])

    def softmax(t, slot, g):
        qi = sched_ref[0, t]
        s = s_ref[slot, :, cols(g)]
        m_old = m_ref[qi, :, cols(g)]
        m = jnp.maximum(m_old, jnp.max(s, axis=0, keepdims=True))
        m_ref[qi, :, cols(g)] = m
        p_ref[slot, :, cols(g)] = jnp.exp2(s - m).astype(BF16)
        return jnp.exp2(m_old - m)

    def values(t, slot, g, alpha):
        t = jnp.maximum(t, 0)
        qi = sched_ref[0, t]
        kv = sched_ref[1, t]
        pv = _dot(vt_ref[sub_blocks * kv], p_ref[slot, :VT_BLK, cols(g)])
        for i in range(1, sub_blocks):
            pv = pv + _dot(vt_ref[sub_blocks * kv + i], p_ref[slot, i * VT_BLK:(i + 1) * VT_BLK, cols(g)])
        acc_ref[qi, :, cols(g)] = alpha * acc_ref[qi, :, cols(g)] + pv

    def ring_steps(j, alphas):
        for u in range(PIPE_SLOTS):
            t = PIPE_SLOTS * j + u
            new_alphas = []
            for g in range(n_groups):
                scores(t + 2, (u + 2) % PIPE_SLOTS, g)
                new_alphas.append(softmax(t, u, g))
                values(t - 1, (u - 1) % PIPE_SLOTS, g, alphas[g])
            alphas = tuple(new_alphas)
        return alphas

    for g in range(n_groups):
        scores(0, 0, g)
        scores(1, 1, g)
    alphas = lax.fori_loop(0, n_steps // PIPE_SLOTS, ring_steps,
                           tuple(jnp.ones((1, COL_GROUP), F32) for _ in range(n_groups)))
    for g in range(n_groups):
        values(n_steps - 1, (n_steps - 1) % PIPE_SLOTS, g, alphas[g])

    lam = _diff_lambda(lq1_ref, lk1_ref, lq2_ref, lk2_ref, lam_init)

    def finish(qi, carry):
        q0 = pl.multiple_of(qi * DQ_BLK, DQ_BLK)
        acc = acc_ref[qi, :SLAB, :] * (1.0 / acc_ref[qi, SLAB:SLAB + 1, :])
        o_t = acc[:, :DQ_BLK] - lam * acc[:, DQ_BLK:]
        o_t = o_t * lax.rsqrt(jnp.mean(o_t * o_t, axis=0, keepdims=True) + RMS_EPS)
        o_t = o_t * subln_ref[...] * (1.0 - lam_init)
        o = o_t.T * sz_ref[pl.ds(q0, DQ_BLK), :].astype(F32)
        o_ref[pl.ds(q0, DQ_BLK), :] = o.astype(BF16)
        return carry

    lax.fori_loop(0, n_q, finish, 0, unroll=True)


def _diff_attn_prompt(qb, kb, vb_t, szb, lq1, lk1, lq2, lk2, subln_col, lam_init):
    b, s, _ = qb.shape
    n_head = BRANCH_W // SLAB
    n_q = s // DQ_BLK
    steps = _diff_attn_schedule(n_q)
    assert len(steps) % PIPE_SLOTS == 0, "one loop body covers PIPE_SLOTS steps"
    assert KV_STEP // CHUNK <= SLAB, "chunk ids of a diagonal block must fit the spare contraction lanes"
    sched = jnp.asarray([[qi for qi, _ in steps], [kv for _, kv in steps]], jnp.int32)
    slab = pl.BlockSpec((None, s, SLAB), lambda bi, hd: (bi, 0, hd))
    small = lambda shape: pl.BlockSpec(shape, lambda bi, hd: (0, 0))
    return pl.pallas_call(
        functools.partial(_diff_attn_prompt_kernel, lam_init=lam_init),
        grid=(b, n_head),
        in_specs=[pl.BlockSpec(memory_space=pltpu.SMEM), slab, slab,
                  pl.BlockSpec((None, None, s // VT_BLK, VT_ROWS, VT_BLK), lambda bi, hd: (bi, hd, 0, 0, 0)),
                  slab,
                  small(lq1.shape), small(lk1.shape), small(lq2.shape), small(lk2.shape), small(subln_col.shape)],
        out_specs=slab,
        out_shape=jax.ShapeDtypeStruct((b, s, BRANCH_W), BF16),
        scratch_shapes=[pltpu.VMEM((n_q, 2 * DQ_BLK, 2 * SLAB), BF16),
                        pltpu.VMEM((2, KV_STEP, SLAB), BF16),
                        pltpu.VMEM((n_q, 1, 2 * DQ_BLK), F32),
                        pltpu.VMEM((n_q, VT_ROWS, 2 * DQ_BLK), F32),
                        pltpu.VMEM((PIPE_SLOTS, KV_STEP, 2 * DQ_BLK), F32),
                        pltpu.VMEM((PIPE_SLOTS, KV_STEP, 2 * DQ_BLK), BF16)],
        compiler_params=pltpu.CompilerParams(dimension_semantics=("arbitrary", "arbitrary"),
                                             vmem_limit_bytes=VMEM_LIMIT),
        name="diff_attn_prompt",
    )(sched, qb, kb, vb_t, szb, lq1, lk1, lq2, lk2, subln_col)


def _mem_kv_kernel(mem_ref, g_ref, w_ref, kf_ref, vf_ref, kb_ref, vb_ref):
    x = mem_ref[...]
    h = (x * lax.rsqrt(jnp.mean(x * x, axis=-1, keepdims=True) + RMS_EPS) * g_ref[...]).astype(BF16)
    k = _dot(h, w_ref[:, :BRANCH_W])
    v = _dot(h, w_ref[:, BRANCH_W:])
    kf_ref[...] = k
    vf_ref[...] = v
    kb_ref[...] = k.astype(BF16)
    vb_ref[...] = v.astype(BF16)


def _mem_kv(mem, norm_g, w_bf16):
    b, n, d = mem.shape
    blk = lambda width: pl.BlockSpec((None, n, width), lambda bi: (bi, 0, 0))
    out = lambda dt: jax.ShapeDtypeStruct((b, n, BRANCH_W), dt)
    return pl.pallas_call(
        _mem_kv_kernel,
        grid=(b,),
        in_specs=[blk(d), pl.BlockSpec((1, d), lambda bi: (0, 0)),
                  pl.BlockSpec((d, 2 * BRANCH_W), lambda bi: (0, 0))],
        out_specs=[blk(BRANCH_W)] * 4,
        out_shape=[out(F32), out(F32), out(BF16), out(BF16)],
        compiler_params=pltpu.CompilerParams(dimension_semantics=("arbitrary",)),
        name="mem_kv",
    )(mem, norm_g, w_bf16)


def _mem_attn_kernel(q_ref, k_ref, v_ref, sz_ref, o_ref):
    for hd in range(BRANCH_W // SLAB):
        cols = slice(hd * SLAB, (hd + 1) * SLAB)
        s = _dot_nt(q_ref[:, cols], k_ref[:, cols].astype(BF16))
        m = jnp.max(s, axis=-1, keepdims=True)
        p = jnp.exp2(s - m)
        l = jnp.sum(p, axis=-1, keepdims=True)
        o = _dot(p.astype(BF16), v_ref[:, cols].astype(BF16)) * (1.0 / l)
        o_ref[:, cols] = (o * sz_ref[:, cols].astype(F32)).astype(BF16)


def _mem_attn(qm, mk, mv, szm):
    b, s, _ = qm.shape
    n = mk.shape[1]
    tq = min(PROJ_ROWS, s)
    row = pl.BlockSpec((None, tq, BRANCH_W), lambda bi, j: (bi, j, 0))
    mem = pl.BlockSpec((None, n, BRANCH_W), lambda bi, j: (bi, 0, 0))
    return pl.pallas_call(
        _mem_attn_kernel,
        grid=(b, s // tq),
        in_specs=[row, mem, mem, row],
        out_specs=row,
        out_shape=jax.ShapeDtypeStruct((b, s, BRANCH_W), BF16),
        compiler_params=pltpu.CompilerParams(dimension_semantics=("arbitrary", "arbitrary")),
        name="mem_attn",
    )(qm, mk, mv, szm)


def _merge_kernel(ga_ref, gb_ref, gm_ref, sg_ref, x_ref, wa_ref, wb_ref, wm_ref, wo_ref, nf_ref, y_ref):
    d = x_ref.shape[1]
    mixed = sg_ref[:, :d].astype(F32) * _dot(ga_ref[...], wa_ref[...])
    mixed = mixed + sg_ref[:, d:2 * d].astype(F32) * _dot(gb_ref[...], wb_ref[...])
    mixed = mixed + sg_ref[:, 2 * d:].astype(F32) * _dot(gm_ref[...], wm_ref[...])
    hsum = x_ref[...] + _dot(mixed.astype(BF16), wo_ref[...])
    y_ref[...] = hsum * lax.rsqrt(jnp.mean(hsum * hsum, axis=-1, keepdims=True) + RMS_EPS) * nf_ref[...]


def _merge(ga, gb, gm, sg, x, wa, wb, wm, wo, norm_final):
    b, s, d = x.shape
    tm = min(PROJ_ROWS, s)
    row = lambda width: pl.BlockSpec((None, tm, width), lambda bi, j: (bi, j, 0))
    full = lambda a: pl.BlockSpec(a.shape, lambda bi, j: (0, 0))
    return pl.pallas_call(
        _merge_kernel,
        grid=(b, s // tm),
        in_specs=[row(BRANCH_W), row(BRANCH_W), row(BRANCH_W), row(sg.shape[2]), row(d),
                  full(wa), full(wb), full(wm), full(wo), full(norm_final)],
        out_specs=row(d),
        out_shape=jax.ShapeDtypeStruct((b, s, d), F32),
        compiler_params=pltpu.CompilerParams(dimension_semantics=("arbitrary", "arbitrary"),
                                             vmem_limit_bytes=VMEM_LIMIT),
        name="merge",
    )(ga, gb, gm, sg, x, wa, wb, wm, wo, norm_final)


def _chunk_attn_sample_kernel(q_ref, k_ref, v_ref, bias_ref, sz_ref, o_ref):
    t = q_ref.shape[0]
    lane = lax.broadcasted_iota(jnp.int32, (t, SLAB), 1)
    for p in range(BRANCH_W // SLAB):
        cols = slice(p * SLAB, (p + 1) * SLAB)
        q2 = _split_pair(q_ref[:, cols])
        s = _dot_nt(q2, k_ref[:, cols]) + bias_ref[p]
        m = jnp.max(s, axis=-1, keepdims=True)
        pr = jnp.exp2(s - m)
        l = jnp.sum(pr, axis=-1, keepdims=True)
        o = _dot(pr.astype(BF16), v_ref[:, cols]) * (1.0 / l)
        o = jnp.where(lane < HALF, o[:t], o[t:])
        o_ref[:, cols] = (o * sz_ref[:, cols].astype(F32)).astype(BF16)


def _chunk_attn_sample(qa, k_cat, v_cat, bias, sza):
    b, t, _ = qa.shape
    n_keys = k_cat.shape[1]
    row = pl.BlockSpec((None, t, BRANCH_W), lambda bi: (bi, 0, 0))
    keys = pl.BlockSpec((None, n_keys, BRANCH_W), lambda bi: (bi, 0, 0))
    return pl.pallas_call(
        _chunk_attn_sample_kernel,
        grid=(b,),
        in_specs=[row, keys, keys, pl.BlockSpec(bias.shape, lambda bi: (0, 0, 0)), row],
        out_specs=row,
        out_shape=jax.ShapeDtypeStruct((b, t, BRANCH_W), BF16),
        compiler_params=pltpu.CompilerParams(dimension_semantics=("arbitrary",)),
        name="chunk_attn_sample",
    )(qa, k_cat, v_cat, bias, sza)


def _diff_attn_sample_kernel(q_ref, kt_ref, vc_ref, kn_ref, vn_ref, nbias_ref, sz_ref,
                             lq1_ref, lk1_ref, lq2_ref, lk2_ref, subln_ref, o_ref, *, lam_init):
    t = q_ref.shape[0]
    past = kt_ref.shape[2]
    lam = _diff_lambda(lq1_ref, lk1_ref, lq2_ref, lk2_ref, lam_init)
    for hd in range(BRANCH_W // SLAB):
        cols = slice(hd * SLAB, (hd + 1) * SLAB)
        q2 = _split_pair(q_ref[:, cols])
        kt = kt_ref[2 * hd:2 * hd + 2].reshape(SLAB, past).astype(BF16)
        s_c = _dot(q2, kt)
        s_n = _dot_nt(q2, kn_ref[:, cols]) + nbias_ref[...]
        m = jnp.maximum(jnp.max(s_c, axis=-1, keepdims=True), jnp.max(s_n, axis=-1, keepdims=True))
        p_c = jnp.exp2(s_c - m)
        p_n = jnp.exp2(s_n - m)
        l = jnp.sum(p_c, axis=-1, keepdims=True) + jnp.sum(p_n, axis=-1, keepdims=True)
        o = _dot(p_c.astype(BF16), vc_ref[:, cols].astype(BF16)) + _dot(p_n.astype(BF16), vn_ref[:, cols])
        o = o * (1.0 / l)
        od = o[:t] - lam * o[t:]
        od = od * lax.rsqrt(jnp.mean(od * od, axis=-1, keepdims=True) + RMS_EPS) * subln_ref[...] * (1.0 - lam_init)
        o_ref[:, cols] = (od * sz_ref[:, cols].astype(F32)).astype(BF16)


def _diff_attn_sample(qb, cache_kt, cache_v, k_new, v_new, nbias, szb, lq1, lk1, lq2, lk2, subln_row, lam_init):
    b, t, _ = qb.shape
    n_new = k_new.shape[1]
    tok = pl.BlockSpec((None, t, BRANCH_W), lambda bi: (bi, 0, 0))
    new = pl.BlockSpec((None, n_new, BRANCH_W), lambda bi: (bi, 0, 0))
    whole = lambda a: pl.BlockSpec((None,) + a.shape[1:], lambda bi: (bi,) + (0,) * (a.ndim - 1))
    small = lambda a: pl.BlockSpec(a.shape, lambda bi: (0, 0))
    return pl.pallas_call(
        functools.partial(_diff_attn_sample_kernel, lam_init=lam_init),
        grid=(b,),
        in_specs=[tok, whole(cache_kt), whole(cache_v), new, new, small(nbias), tok,
                  small(lq1), small(lk1), small(lq2), small(lk2), small(subln_row)],
        out_specs=tok,
        out_shape=jax.ShapeDtypeStruct((b, t, BRANCH_W), BF16),
        compiler_params=pltpu.CompilerParams(dimension_semantics=("arbitrary",), vmem_limit_bytes=VMEM_LIMIT),
        name="diff_attn_sample",
    )(qb, cache_kt, cache_v, k_new, v_new, nbias, szb, lq1, lk1, lq2, lk2, subln_row)


def _rope_tables(pos):
    inv = 1.0 / (ROPE_THETA ** (jnp.arange(0, HALF, 2, dtype=F32) / HALF))
    ang = pos.astype(F32)[:, None] * inv[None, :]
    cos = jnp.cos(ang)
    sin = jnp.sin(ang)
    return jnp.tile(cos, (1, SLAB // (HALF // 2))), jnp.tile(jnp.concatenate([-sin, sin], axis=1), (1, SLAB // HALF))


def _bias_lookup(rb_ref, pair, dist, head_in_pair):
    idx = jnp.clip(dist, -REL_CLIP, REL_CLIP) + REL_CLIP

    def entry(k, acc):
        v = jnp.where(head_in_pair == 1, rb_ref[2 * pair + 1, k], rb_ref[2 * pair, k])
        return jnp.where(idx == k, v, acc)

    return LOG2E * lax.fori_loop(0, 2 * REL_CLIP + 1, entry, jnp.zeros(dist.shape, F32))


def _prompt_bias_kernel(rb_ref, o_ref):
    pair = pl.program_id(0)

    def rows(u0, n):
        u = u0 + lax.broadcasted_iota(jnp.int32, (n, 2 * Q_BLK), 0)
        c = lax.broadcasted_iota(jnp.int32, (n, 2 * Q_BLK), 1)
        ql = c & (Q_BLK - 1)
        visible = ((ql < CHUNK) & (u < A_PAST + CHUNK)) | ((ql >= CHUNK) & (u >= CHUNK))
        return ql + A_PAST - u, jnp.where(c >= Q_BLK, 1, 0), visible

    dist, head_in_pair, visible = rows(BAND_CONST, BAND - BAND_CONST)
    o_ref[BAND_CONST:BAND, :] = jnp.where(visible, _bias_lookup(rb_ref, pair, dist, head_in_pair), NEG_INF)
    _, head_in_pair, visible = rows(0, BAND_CONST)
    far = LOG2E * jnp.where(head_in_pair == 1, rb_ref[2 * pair + 1, 2 * REL_CLIP], rb_ref[2 * pair, 2 * REL_CLIP])
    o_ref[:BAND_CONST, :] = jnp.where(visible, far, NEG_INF)
    o_ref[BAND:, :] = jnp.full((o_ref.shape[0] - BAND, 2 * Q_BLK), NEG_INF, F32)


def _prompt_bias_table(rel_bias):
    n_pair = rel_bias.shape[0] // 2
    rows = BAND + A_PAST
    return pl.pallas_call(
        _prompt_bias_kernel,
        grid=(n_pair,),
        in_specs=[pl.BlockSpec(memory_space=pltpu.SMEM)],
        out_specs=pl.BlockSpec((None, rows, 2 * Q_BLK), lambda p: (p, 0, 0)),
        out_shape=jax.ShapeDtypeStruct((n_pair, rows, 2 * Q_BLK), F32),
        compiler_params=pltpu.CompilerParams(dimension_semantics=("arbitrary",)),
        name="prompt_bias",
    )(rel_bias)


def _sample_bias_kernel(rb_ref, o_ref, *, p_len, t):
    pair = pl.program_id(0)
    r = lax.broadcasted_iota(jnp.int32, o_ref.shape, 0)
    j = lax.broadcasted_iota(jnp.int32, o_ref.shape, 1)
    head_in_pair = jnp.where(r >= t, 1, 0)
    dist = p_len + r - t * head_in_pair - j
    o_ref[...] = jnp.where(j < p_len + t, _bias_lookup(rb_ref, pair, dist, head_in_pair), NEG_INF)


def _sample_bias_table(rel_bias, p_len, t, n_keys):
    n_pair = rel_bias.shape[0] // 2
    return pl.pallas_call(
        functools.partial(_sample_bias_kernel, p_len=p_len, t=t),
        grid=(n_pair,),
        in_specs=[pl.BlockSpec(memory_space=pltpu.SMEM)],
        out_specs=pl.BlockSpec((None, 2 * t, n_keys), lambda p: (p, 0, 0)),
        out_shape=jax.ShapeDtypeStruct((n_pair, 2 * t, n_keys), F32),
        compiler_params=pltpu.CompilerParams(dimension_semantics=("arbitrary",)),
        name="sample_bias",
    )(rel_bias)


def _round_up(n, m):
    return (n + m - 1) // m * m


def kernel(x_prompt, x_sample, cache_a_k, cache_a_v, cache_b_k, cache_b_v, cache_mem_k, cache_mem_v, mem_prompt, norm_in, w_in, rel_bias, lambda_q1, lambda_k1, lambda_q2, lambda_k2, subln, norm_mem, w_mem_kv, w_branch_a, w_branch_b, w_branch_m, w_out, norm_final):
    depth = w_in.shape[0]
    assert depth == 1, "kernels are written for the single-layer step"
    bsz, seq, d_model = x_prompt.shape
    dec_b, dec_t, _ = x_sample.shape
    past = cache_b_k.shape[2]
    a_cache = cache_a_k.shape[2]
    n_mem = mem_prompt.shape[1]
    keep = min(A_PAST, seq)
    lam_init = 0.8 - 0.6 * math.exp(-0.3 * 0)
    l = 0

    w_in_b = w_in[l].astype(BF16)
    w_mem_b = w_mem_kv[l].astype(BF16)
    wa, wb, wm, wo = (w[l].astype(BF16) for w in (w_branch_a, w_branch_b, w_branch_m, w_out))
    g_in = norm_in[l][None, :]
    g_mem = norm_mem[l][None, :]
    g_final = norm_final[None, :]
    lq1, lk1, lq2, lk2 = (v[l][None, :] for v in (lambda_q1, lambda_k1, lambda_q2, lambda_k2))

    cos_p, sin_p = _rope_tables(jnp.arange(seq))
    (qa, ka, va_t, ka_keep, va_keep, qb, kb, kb_f32, vb_t, vb_f32, qm, sza, szb, szm, sg) = _in_proj(
        x_prompt, g_in, w_in_b, cos_p, sin_p, keep=keep, transposed_v=True)
    ga = _chunk_attn_prompt(qa, ka, va_t, _prompt_bias_table(rel_bias[l]), sza)
    gb = _diff_attn_prompt(qb, kb, vb_t, szb, lq1, lk1, lq2, lk2, subln[l][:, None], lam_init)
    mk_f32, mv_f32, mk, mv = _mem_kv(mem_prompt, g_mem, w_mem_b)
    gm = _mem_attn(qm, mk, mv, szm)
    y_prompt = _merge(ga, gb, gm, sg, x_prompt, wa, wb, wm, wo, g_final)

    n_tok = dec_b * dec_t
    pos_s = past + jnp.tile(jnp.arange(dec_t), dec_b)
    cos_s, sin_s = _rope_tables(pos_s)
    xs = x_sample.reshape(1, n_tok, d_model)
    (qa_s, ka_s, va_s, ka_s32, va_s32, qb_s, kb_s, kb_s32, vb_s, vb_s32, qm_s, sza_s, szb_s, szm_s, sg_s) = _in_proj(
        xs, g_in, w_in_b, cos_s, sin_s, keep=n_tok, transposed_v=False)
    per_req = lambda a: a.reshape(dec_b, dec_t, a.shape[-1])

    n_keys = _round_up(a_cache + dec_t, SLAB)
    pad_a = ((0, 0), (0, n_keys - a_cache - dec_t), (0, 0))
    k_cat = jnp.pad(jnp.concatenate([cache_a_k[l].reshape(dec_b, a_cache, BRANCH_W).astype(BF16), per_req(ka_s)], axis=1), pad_a)
    v_cat = jnp.pad(jnp.concatenate([cache_a_v[l].reshape(dec_b, a_cache, BRANCH_W).astype(BF16), per_req(va_s)], axis=1), pad_a)
    ga_s = _chunk_attn_sample(per_req(qa_s), k_cat, v_cat, _sample_bias_table(rel_bias[l], a_cache, dec_t, n_keys),
                              per_req(sza_s))

    n_new = _round_up(dec_t, SLAB)
    pad_b = ((0, 0), (0, n_new - dec_t), (0, 0))
    nbias = jnp.where(jnp.arange(n_new) < dec_t, 0.0, NEG_INF).astype(F32)[None, :]
    gb_s = _diff_attn_sample(per_req(qb_s), jnp.transpose(cache_b_k[l], (0, 2, 3, 1)),
                             cache_b_v[l].reshape(dec_b, past, BRANCH_W),
                             jnp.pad(per_req(kb_s), pad_b), jnp.pad(per_req(vb_s), pad_b), nbias, per_req(szb_s),
                             lq1, lk1, lq2, lk2, subln[l][None, :], lam_init)
    gm_s = _mem_attn(per_req(qm_s), cache_mem_k[l].reshape(dec_b, n_mem, BRANCH_W),
                     cache_mem_v[l].reshape(dec_b, n_mem, BRANCH_W), per_req(szm_s))
    flat = lambda a: a.reshape(1, n_tok, a.shape[-1])
    y_sample = _merge(flat(ga_s), flat(gb_s), flat(gm_s), sg_s, xs, wa, wb, wm, wo, g_final).reshape(x_sample.shape)

    a_heads, a_dim = cache_a_k.shape[3:]
    bk_heads, bk_dim = cache_b_k.shape[3:]
    bv_heads, bv_dim = cache_b_v.shape[3:]
    m_heads, m_dim = cache_mem_k.shape[3:]
    return (y_prompt, y_sample,
            ka_keep.reshape(1, bsz, keep, a_heads, a_dim), va_keep.reshape(1, bsz, keep, a_heads, a_dim),
            kb_f32.reshape(1, bsz, seq, bk_heads, bk_dim), vb_f32.reshape(1, bsz, seq, bv_heads, bv_dim),
            mk_f32.reshape(1, bsz, n_mem, m_heads, m_dim), mv_f32.reshape(1, bsz, n_mem, m_heads, m_dim),
            ka_s32.reshape(1, dec_b, dec_t, a_heads, a_dim), va_s32.reshape(1, dec_b, dec_t, a_heads, a_dim),
            kb_s32.reshape(1, dec_b, dec_t, bk_heads, bk_dim), vb_s32.reshape(1, dec_b, dec_t, bv_heads, bv_dim))
```

```python
import functools
import math

import jax
import jax.numpy as jnp
from jax import lax
from jax.experimental import pallas as pl
from jax.experimental.pallas import tpu as pltpu

F32 = jnp.float32
BF16 = jnp.bfloat16

CHUNK = 64
CHUNK_SHIFT = 6
A_PAST = 8 * CHUNK
REL_CLIP = 128
BRANCH_W = 512
SLAB = 128
HALF = 64
ROPE_THETA = 10000.0
RMS_EPS = 1e-6
NEG_INF = -1e30

PROJ_ROWS = 512
VT_BLK = 256
VT_ROWS = SLAB + 16
KV_STEP = 512
Q_BLK = 128
BAND = A_PAST + Q_BLK
BAND_CONST = A_PAST - REL_CLIP
DQ_BLK = 512
LOG2E = math.log2(math.e)
PIPE_SLOTS = 3
COL_GROUP = 256
BODY_STEPS = PIPE_SLOTS
CHUNK_UNROLL = 32
assert DQ_BLK == KV_STEP

VMEM_LIMIT = 56 * 1024 * 1024


def _dot(a, b):
    return jnp.dot(a, b, preferred_element_type=F32)


def _dot_nt(a, b):
    return lax.dot_general(a, b, (((1,), (1,)), ((), ())), preferred_element_type=F32)


def _silu(z):
    return z * (1.0 / (1.0 + jnp.exp(-z)))


def _sigmoid(z):
    return 1.0 / (1.0 + jnp.exp(-z))


def _split_pair(q):
    lane = lax.broadcasted_iota(jnp.int32, q.shape, 1)
    zero = jnp.zeros_like(q)
    return jnp.concatenate([jnp.where(lane < HALF, q, zero), jnp.where(lane >= HALF, q, zero)], axis=0)


def _diff_lambda(lq1_ref, lk1_ref, lq2_ref, lk2_ref, lam_init):
    e1 = jnp.exp(jnp.sum(lq1_ref[...] * lk1_ref[...], axis=-1, keepdims=True))
    e2 = jnp.exp(jnp.sum(lq2_ref[...] * lk2_ref[...], axis=-1, keepdims=True))
    return e1 - e2 + lam_init


def _in_proj_kernel(x_ref, g_ref, w_ref, cos_ref, sin_ref,
                    qa_ref, ka_ref, va_ref, kaw_ref, vaw_ref,
                    qb_ref, kb_ref, kbf_ref, vb_ref, vbf_ref,
                    qm_ref, sza_ref, szb_ref, szm_ref, sg_ref, *, transposed_v, mem_scale):
    tm = x_ref.shape[0]
    x = x_ref[...]
    h = (x * lax.rsqrt(jnp.mean(x * x, axis=-1, keepdims=True) + RMS_EPS) * g_ref[...]).astype(BF16)

    def proj(idx):
        return _dot(h, w_ref[:, idx * BRANCH_W:(idx + 1) * BRANCH_W])

    cos = cos_ref[...]
    sin = sin_ref[...]
    lane = lax.broadcasted_iota(jnp.int32, (tm, SLAB), 1)
    first_half = (lane & (HALF // 2)) == 0

    def rope_slabs(acc):
        out = []
        for j in range(BRANCH_W // SLAB):
            y = acc[:, j * SLAB:(j + 1) * SLAB]
            partner = jnp.where(first_half, pltpu.roll(y, SLAB - HALF // 2, 1), pltpu.roll(y, HALF // 2, 1))
            out.append(y * cos + partner * sin)
        return out

    qa_ref[...] = (proj(0) * (LOG2E * HALF ** -0.5)).astype(BF16)
    ak = proj(1)
    ka_ref[...] = ak.astype(BF16)
    kaw_ref[...] = ak
    av = proj(2)
    vaw_ref[...] = av
    if transposed_v:
        for p in range(BRANCH_W // SLAB):
            for i in range(tm // SLAB):
                va_ref[p, i] = av[i * SLAB:(i + 1) * SLAB, p * SLAB:(p + 1) * SLAB].T.astype(BF16)
    else:
        va_ref[...] = av.astype(BF16)
    sza_ref[...] = _silu(proj(3)).astype(BF16)

    for j, y in enumerate(rope_slabs(proj(4))):
        qb_ref[:, j * SLAB:(j + 1) * SLAB] = (y * (LOG2E * HALF ** -0.5)).astype(BF16)
    for j, y in enumerate(rope_slabs(proj(5))):
        kbf_ref[:, j * SLAB:(j + 1) * SLAB] = y
        kb_ref[:, j * SLAB:(j + 1) * SLAB] = y.astype(BF16)
    bv = proj(6)
    for hd in range(BRANCH_W // SLAB):
        vbf_ref[:, hd, :] = bv[:, hd * SLAB:(hd + 1) * SLAB]
    if transposed_v:
        for hd in range(BRANCH_W // SLAB):
            for i in range(tm // VT_BLK):
                vb_ref[hd, i, :SLAB, :] = bv[i * VT_BLK:(i + 1) * VT_BLK, hd * SLAB:(hd + 1) * SLAB].T.astype(BF16)
                vb_ref[hd, i, SLAB:, :] = jnp.ones((VT_ROWS - SLAB, VT_BLK), BF16)
    else:
        vb_ref[...] = bv.astype(BF16)
    szb_ref[...] = _silu(proj(7)).astype(BF16)

    qm_ref[...] = (proj(8) * mem_scale).astype(BF16)
    szm_ref[...] = _silu(proj(9)).astype(BF16)
    for j in range(sg_ref.shape[1] // BRANCH_W):
        sg_ref[:, j * BRANCH_W:(j + 1) * BRANCH_W] = _sigmoid(proj(10 + j)).astype(BF16)


def _in_proj(x, norm_g, w_bf16, cos_tab, sin_tab, *, keep, transposed_v):
    b, s, d = x.shape
    d_in = w_bf16.shape[1]
    tm = min(PROJ_ROWS, s)
    nt = s // tm
    first_kept = (s - keep) // tm
    n_gate = d_in - 10 * BRANCH_W

    row = lambda bi, j: (bi, j, 0)
    kept = lambda bi, j: (bi, jnp.maximum(j - first_kept, 0), 0)
    act = lambda dt: jax.ShapeDtypeStruct((b, s, BRANCH_W), dt)
    act_spec = pl.BlockSpec((None, tm, BRANCH_W), row)
    win = jax.ShapeDtypeStruct((b, keep, BRANCH_W), F32)
    win_spec = pl.BlockSpec((None, tm, BRANCH_W), kept)
    if transposed_v:
        va = jax.ShapeDtypeStruct((b, BRANCH_W // SLAB, s // SLAB, SLAB, SLAB), BF16)
        va_spec = pl.BlockSpec((None, BRANCH_W // SLAB, tm // SLAB, SLAB, SLAB), lambda bi, j: (bi, 0, j, 0, 0))
        vb = jax.ShapeDtypeStruct((b, BRANCH_W // SLAB, s // VT_BLK, VT_ROWS, VT_BLK), BF16)
        vb_spec = pl.BlockSpec((None, BRANCH_W // SLAB, tm // VT_BLK, VT_ROWS, VT_BLK), lambda bi, j: (bi, 0, j, 0, 0))
    else:
        va, va_spec, vb, vb_spec = act(BF16), act_spec, act(BF16), act_spec

    out_shape = [act(BF16), act(BF16), va, win, win,
                 act(BF16), act(BF16), act(F32), vb, jax.ShapeDtypeStruct((b, s, BRANCH_W // SLAB, SLAB), F32),
                 act(BF16), act(BF16), act(BF16), act(BF16),
                 jax.ShapeDtypeStruct((b, s, n_gate), BF16)]
    out_specs = [act_spec, act_spec, va_spec, win_spec, win_spec,
                 act_spec, act_spec, act_spec, vb_spec,
                 pl.BlockSpec((None, tm, BRANCH_W // SLAB, SLAB), lambda bi, j: (bi, j, 0, 0)),
                 act_spec, act_spec, act_spec, act_spec,
                 pl.BlockSpec((None, tm, n_gate), row)]
    return pl.pallas_call(
        functools.partial(_in_proj_kernel, transposed_v=transposed_v, mem_scale=LOG2E * SLAB ** -0.5),
        grid=(b, nt),
        in_specs=[pl.BlockSpec((None, tm, d), row),
                  pl.BlockSpec((1, d), lambda bi, j: (0, 0)),
                  pl.BlockSpec((d, d_in), lambda bi, j: (0, 0), pipeline_mode=pl.Buffered(1)),
                  pl.BlockSpec((tm, SLAB), lambda bi, j: (j, 0)),
                  pl.BlockSpec((tm, SLAB), lambda bi, j: (j, 0))],
        out_specs=out_specs,
        out_shape=out_shape,
        compiler_params=pltpu.CompilerParams(dimension_semantics=("arbitrary", "arbitrary"),
                                             vmem_limit_bytes=VMEM_LIMIT),
        name="in_proj",
    )(x, norm_g, w_bf16, cos_tab, sin_tab)


def _chunk_attn_prompt_kernel(q_ref, k_ref, vt_ref, bias_ref, sz_ref, o_ref):
    s_len = q_ref.shape[0]

    def q_block(g, carry):
        q0 = pl.multiple_of(g * Q_BLK, Q_BLK)
        q2 = _split_pair(q_ref[pl.ds(q0, Q_BLK), :])
        start = g * Q_BLK - A_PAST
        start_c = jnp.maximum(start, 0)
        off = pl.multiple_of(start_c - start, SLAB)
        start_c = pl.multiple_of(start_c, SLAB)
        kb = k_ref[pl.ds(start_c, BAND), :]
        s = _dot_nt(kb, q2) + bias_ref[pl.ds(off, BAND), :]
        m = jnp.max(s, axis=0, keepdims=True)
        p = jnp.exp2(s - m)
        l = jnp.sum(p, axis=0, keepdims=True)
        pb = p.astype(BF16)
        blk0 = start_c // SLAB
        acc = jnp.zeros((SLAB, 2 * Q_BLK), F32)
        for i in range(BAND // SLAB):
            acc = acc + _dot(vt_ref[blk0 + i], pb[i * SLAB:(i + 1) * SLAB, :])
        acc = acc * (1.0 / l)
        o_t = jnp.concatenate([acc[:HALF, :Q_BLK], acc[HALF:, Q_BLK:]], axis=0)
        o = o_t.T * sz_ref[pl.ds(q0, Q_BLK), :].astype(F32)
        o_ref[pl.ds(q0, Q_BLK), :] = o.astype(BF16)
        return carry

    lax.fori_loop(0, s_len // Q_BLK, q_block, 0, unroll=CHUNK_UNROLL)


def _chunk_attn_prompt(qa, ka, va_t, bias_t, sza):
    b, s, _ = qa.shape
    n_pair = BRANCH_W // SLAB
    slab = pl.BlockSpec((None, s, SLAB), lambda bi, p: (bi, 0, p))
    return pl.pallas_call(
        _chunk_attn_prompt_kernel,
        grid=(b, n_pair),
        in_specs=[slab, slab,
                  pl.BlockSpec((None, None, s // SLAB, SLAB, SLAB), lambda bi, p: (bi, p, 0, 0, 0)),
                  pl.BlockSpec((None, BAND + A_PAST, 2 * Q_BLK), lambda bi, p: (p, 0, 0)),
                  slab],
        out_specs=slab,
        out_shape=jax.ShapeDtypeStruct((b, s, BRANCH_W), BF16),
        compiler_params=pltpu.CompilerParams(dimension_semantics=("arbitrary", "arbitrary"),
                                             vmem_limit_bytes=VMEM_LIMIT),
        name="chunk_attn_prompt",
    )(qa, ka, va_t, bias_t, sza)


def _diff_attn_schedule(n_q):
    steps = [(qi, qi) for qi in range(n_q)]
    steps += [(qi, kv) for kv in range(n_q) for qi in range(kv + 1, n_q)]
    return steps


def _diff_attn_prompt_kernel(sched_ref, q_ref, k_ref, vt_ref, sz_ref, lq1_ref, lk1_ref, lq2_ref, lk2_ref, subln_ref,
                             o_ref, q2_ref, kz_ref, m_ref, acc_ref, s_ref, p_ref, *, lam_init):
    n_q = q_ref.shape[0] // DQ_BLK
    n_steps = sched_ref.shape[1]
    sub_blocks = KV_STEP // VT_BLK

    @pl.when((pl.program_id(0) == 0) & (pl.program_id(1) == 0))
    def _():
        k_chunk = jnp.right_shift(lax.broadcasted_iota(jnp.int32, kz_ref.shape[1:], 0), CHUNK_SHIFT)
        chunk_id = lax.broadcasted_iota(jnp.int32, kz_ref.shape[1:], 1)
        kz_ref[0] = jnp.zeros(kz_ref.shape[1:], BF16)
        kz_ref[1] = jnp.where(k_chunk == chunk_id, 1.0, 0.0).astype(BF16)
        q_col = lax.broadcasted_iota(jnp.int32, (SLAB, 2 * DQ_BLK), 1)
        q_chunk = jnp.right_shift(q_col & (DQ_BLK - 1), CHUNK_SHIFT)
        hidden = lax.broadcasted_iota(jnp.int32, (SLAB, 2 * DQ_BLK), 0) > q_chunk
        for qi in range(n_q):
            q2_ref[qi, SLAB:, :] = jnp.where(hidden, NEG_INF, 0.0).astype(BF16)

    dim = lax.broadcasted_iota(jnp.int32, (SLAB, DQ_BLK), 0)
    for qi in range(n_q):
        q_t = q_ref[qi * DQ_BLK:(qi + 1) * DQ_BLK, :].astype(F32).T.astype(BF16)
        q2_ref[qi, :SLAB, :DQ_BLK] = jnp.where(dim < HALF, q_t, jnp.zeros_like(q_t))
        q2_ref[qi, :SLAB, DQ_BLK:] = jnp.where(dim >= HALF, q_t, jnp.zeros_like(q_t))
    m_ref[...] = jnp.full(m_ref.shape, NEG_INF, F32)
    acc_ref[...] = jnp.zeros(acc_ref.shape, F32)
    p_ref[PIPE_SLOTS - 1] = jnp.zeros(p_ref.shape[1:], BF16)

    n_groups = 2 * DQ_BLK // COL_GROUP

    def cols(g):
        return slice(g * COL_GROUP, (g + 1) * COL_GROUP)

    def scores(t, slot, g):
        t = jnp.minimum(t, n_steps - 1)
        qi = sched_ref[0, t]
        kv = sched_ref[1, t]
        k0 = pl.multiple_of(kv * KV_STEP, KV_STEP)
        keys = jnp.concatenate([k_ref[pl.ds(k0, KV_STEP), :], kz_ref[jnp.where(qi == kv, 1, 0)]], axis=1)
        s_ref[slot, :, cols(g)] = _dot(keys, q2_ref[qi, :, cols(g)])

    def softmax(t, slot, g):
        qi = sched_ref[0, t]
        s = s_ref[slot, :, cols(g)]
        m_old = m_ref[qi, :, cols(g)]
        m = jnp.maximum(m_old, jnp.max(s, axis=0, keepdims=True))
        m_ref[qi, :, cols(g)] = m
        p_ref[slot, :, cols(g)] = jnp.exp2(s - m).astype(BF16)
        return jnp.exp2(m_old - m)

    def values(t, slot, g, alpha):
        t = jnp.maximum(t, 0)
        qi = sched_ref[0, t]
        kv = sched_ref[1, t]
        pv = _dot(vt_ref[sub_blocks * kv], p_ref[slot, :VT_BLK, cols(g)])
        for i in range(1, sub_blocks):
            pv = pv + _dot(vt_ref[sub_blocks * kv + i], p_ref[slot, i * VT_BLK:(i + 1) * VT_BLK, cols(g)])
        acc_ref[qi, :, cols(g)] = alpha * acc_ref[qi, :, cols(g)] + pv

    def ring_steps(j, alphas):
        for u in range(BODY_STEPS):
            t = BODY_STEPS * j + u
            new_alphas = []
            for g in range(n_groups):
                scores(t + 2, (u + 2) % PIPE_SLOTS, g)
                new_alphas.append(softmax(t, u % PIPE_SLOTS, g))
                values(t - 1, (u - 1) % PIPE_SLOTS, g, alphas[g])
            alphas = tuple(new_alphas)
        return alphas

    for g in range(n_groups):
        scores(0, 0, g)
        scores(1, 1, g)
    alphas = lax.fori_loop(0, n_steps // BODY_STEPS, ring_steps,
                           tuple(jnp.ones((1, COL_GROUP), F32) for _ in range(n_groups)))
    for g in range(n_groups):
        values(n_steps - 1, (n_steps - 1) % PIPE_SLOTS, g, alphas[g])

    lam = _diff_lambda(lq1_ref, lk1_ref, lq2_ref, lk2_ref, lam_init)

    def finish(qi, carry):
        q0 = pl.multiple_of(qi * DQ_BLK, DQ_BLK)
        acc = acc_ref[qi, :SLAB, :] * (1.0 / acc_ref[qi, SLAB:SLAB + 1, :])
        o_t = acc[:, :DQ_BLK] - lam * acc[:, DQ_BLK:]
        o_t = o_t * lax.rsqrt(jnp.mean(o_t * o_t, axis=0, keepdims=True) + RMS_EPS)
        o_t = o_t * subln_ref[...] * (1.0 - lam_init)
        o = o_t.T * sz_ref[pl.ds(q0, DQ_BLK), :].astype(F32)
        o_ref[pl.ds(q0, DQ_BLK), :] = o.astype(BF16)
        return carry

    lax.fori_loop(0, n_q, finish, 0, unroll=True)


def _diff_attn_prompt(qb, kb, vb_t, szb, lq1, lk1, lq2, lk2, subln_col, lam_init):
    b, s, _ = qb.shape
    n_head = BRANCH_W // SLAB
    n_q = s // DQ_BLK
    steps = _diff_attn_schedule(n_q)
    assert len(steps) % BODY_STEPS == 0, "one loop body covers BODY_STEPS steps"
    assert KV_STEP // CHUNK <= SLAB, "chunk ids of a diagonal block must fit the spare contraction lanes"
    sched = jnp.asarray([[qi for qi, _ in steps], [kv for _, kv in steps]], jnp.int32)
    slab = pl.BlockSpec((None, s, SLAB), lambda bi, hd: (bi, 0, hd))
    small = lambda shape: pl.BlockSpec(shape, lambda bi, hd: (0, 0))
    return pl.pallas_call(
        functools.partial(_diff_attn_prompt_kernel, lam_init=lam_init),
        grid=(b, n_head),
        in_specs=[pl.BlockSpec(memory_space=pltpu.SMEM), slab, slab,
                  pl.BlockSpec((None, None, s // VT_BLK, VT_ROWS, VT_BLK), lambda bi, hd: (bi, hd, 0, 0, 0)),
                  slab,
                  small(lq1.shape), small(lk1.shape), small(lq2.shape), small(lk2.shape), small(subln_col.shape)],
        out_specs=slab,
        out_shape=jax.ShapeDtypeStruct((b, s, BRANCH_W), BF16),
        scratch_shapes=[pltpu.VMEM((n_q, 2 * SLAB, 2 * DQ_BLK), BF16),
                        pltpu.VMEM((2, KV_STEP, SLAB), BF16),
                        pltpu.VMEM((n_q, 1, 2 * DQ_BLK), F32),
                        pltpu.VMEM((n_q, VT_ROWS, 2 * DQ_BLK), F32),
                        pltpu.VMEM((PIPE_SLOTS, KV_STEP, 2 * DQ_BLK), F32),
                        pltpu.VMEM((PIPE_SLOTS, KV_STEP, 2 * DQ_BLK), BF16)],
        compiler_params=pltpu.CompilerParams(dimension_semantics=("arbitrary", "arbitrary"),
                                             vmem_limit_bytes=VMEM_LIMIT),
        name="diff_attn_prompt",
    )(sched, qb, kb, vb_t, szb, lq1, lk1, lq2, lk2, subln_col)


def _mem_kv_kernel(mem_ref, g_ref, w_ref, kf_ref, vf_ref, kb_ref, vb_ref):
    x = mem_ref[...]
    h = (x * lax.rsqrt(jnp.mean(x * x, axis=-1, keepdims=True) + RMS_EPS) * g_ref[...]).astype(BF16)
    k = _dot(h, w_ref[:, :BRANCH_W])
    v = _dot(h, w_ref[:, BRANCH_W:])
    kf_ref[...] = k
    vf_ref[...] = v
    kb_ref[...] = k.astype(BF16)
    vb_ref[...] = v.astype(BF16)


def _mem_kv(mem, norm_g, w_bf16):
    b, n, d = mem.shape
    blk = lambda width: pl.BlockSpec((None, n, width), lambda bi: (bi, 0, 0))
    out = lambda dt: jax.ShapeDtypeStruct((b, n, BRANCH_W), dt)
    return pl.pallas_call(
        _mem_kv_kernel,
        grid=(b,),
        in_specs=[blk(d), pl.BlockSpec((1, d), lambda bi: (0, 0)),
                  pl.BlockSpec((d, 2 * BRANCH_W), lambda bi: (0, 0))],
        out_specs=[blk(BRANCH_W)] * 4,
        out_shape=[out(F32), out(F32), out(BF16), out(BF16)],
        compiler_params=pltpu.CompilerParams(dimension_semantics=("arbitrary",)),
        name="mem_kv",
    )(mem, norm_g, w_bf16)


def _mem_attn_kernel(q_ref, k_ref, v_ref, sz_ref, o_ref):
    for hd in range(BRANCH_W // SLAB):
        cols = slice(hd * SLAB, (hd + 1) * SLAB)
        s = _dot_nt(q_ref[:, cols], k_ref[:, cols].astype(BF16))
        m = jnp.max(s, axis=-1, keepdims=True)
        p = jnp.exp2(s - m).astype(BF16)
        v = v_ref[:, cols].astype(BF16)
        ol = _dot(p, jnp.concatenate([v, jnp.ones_like(v)], axis=1))
        o = ol[:, :SLAB] / ol[:, SLAB:]
        o_ref[:, cols] = (o * sz_ref[:, cols].astype(F32)).astype(BF16)


def _mem_attn(qm, mk, mv, szm):
    b, s, _ = qm.shape
    n = mk.shape[1]
    tq = min(PROJ_ROWS, s)
    row = pl.BlockSpec((None, tq, BRANCH_W), lambda bi, j: (bi, j, 0))
    mem = pl.BlockSpec((None, n, BRANCH_W), lambda bi, j: (bi, 0, 0))
    return pl.pallas_call(
        _mem_attn_kernel,
        grid=(b, s // tq),
        in_specs=[row, mem, mem, row],
        out_specs=row,
        out_shape=jax.ShapeDtypeStruct((b, s, BRANCH_W), BF16),
        compiler_params=pltpu.CompilerParams(dimension_semantics=("arbitrary", "arbitrary")),
        name="mem_attn",
    )(qm, mk, mv, szm)


def _merge_kernel(ga_ref, gb_ref, gm_ref, sg_ref, x_ref, wa_ref, wb_ref, wm_ref, wo_ref, nf_ref, y_ref):
    d = x_ref.shape[1]
    mixed = sg_ref[:, :d].astype(F32) * _dot(ga_ref[...], wa_ref[...])
    mixed = mixed + sg_ref[:, d:2 * d].astype(F32) * _dot(gb_ref[...], wb_ref[...])
    mixed = mixed + sg_ref[:, 2 * d:].astype(F32) * _dot(gm_ref[...], wm_ref[...])
    hsum = x_ref[...] + _dot(mixed.astype(BF16), wo_ref[...])
    y_ref[...] = hsum * lax.rsqrt(jnp.mean(hsum * hsum, axis=-1, keepdims=True) + RMS_EPS) * nf_ref[...]


def _merge(ga, gb, gm, sg, x, wa, wb, wm, wo, norm_final):
    b, s, d = x.shape
    tm = min(PROJ_ROWS, s)
    row = lambda width: pl.BlockSpec((None, tm, width), lambda bi, j: (bi, j, 0))
    full = lambda a: pl.BlockSpec(a.shape, lambda bi, j: (0, 0))
    return pl.pallas_call(
        _merge_kernel,
        grid=(b, s // tm),
        in_specs=[row(BRANCH_W), row(BRANCH_W), row(BRANCH_W), row(sg.shape[2]), row(d),
                  full(wa), full(wb), full(wm), full(wo), full(norm_final)],
        out_specs=row(d),
        out_shape=jax.ShapeDtypeStruct((b, s, d), F32),
        compiler_params=pltpu.CompilerParams(dimension_semantics=("arbitrary", "arbitrary"),
                                             vmem_limit_bytes=VMEM_LIMIT),
        name="merge",
    )(ga, gb, gm, sg, x, wa, wb, wm, wo, norm_final)


def _chunk_attn_sample_kernel(q_ref, k_ref, v_ref, bias_ref, sz_ref, o_ref):
    t = q_ref.shape[0]
    lane = lax.broadcasted_iota(jnp.int32, (t, SLAB), 1)
    for p in range(BRANCH_W // SLAB):
        cols = slice(p * SLAB, (p + 1) * SLAB)
        q2 = _split_pair(q_ref[:, cols])
        s = _dot_nt(q2, k_ref[:, cols]) + bias_ref[p]
        m = jnp.max(s, axis=-1, keepdims=True)
        pr = jnp.exp2(s - m)
        l = jnp.sum(pr, axis=-1, keepdims=True)
        o = _dot(pr.astype(BF16), v_ref[:, cols]) * (1.0 / l)
        o = jnp.where(lane < HALF, o[:t], o[t:])
        o_ref[:, cols] = (o * sz_ref[:, cols].astype(F32)).astype(BF16)


def _chunk_attn_sample(qa, k_cat, v_cat, bias, sza):
    b, t, _ = qa.shape
    n_keys = k_cat.shape[1]
    row = pl.BlockSpec((None, t, BRANCH_W), lambda bi: (bi, 0, 0))
    keys = pl.BlockSpec((None, n_keys, BRANCH_W), lambda bi: (bi, 0, 0))
    return pl.pallas_call(
        _chunk_attn_sample_kernel,
        grid=(b,),
        in_specs=[row, keys, keys, pl.BlockSpec(bias.shape, lambda bi: (0, 0, 0)), row],
        out_specs=row,
        out_shape=jax.ShapeDtypeStruct((b, t, BRANCH_W), BF16),
        compiler_params=pltpu.CompilerParams(dimension_semantics=("arbitrary",)),
        name="chunk_attn_sample",
    )(qa, k_cat, v_cat, bias, sza)


def _diff_attn_sample_kernel(q_ref, kt_ref, vc_ref, kn_ref, vn_ref, nbias_ref, sz_ref,
                             lq1_ref, lk1_ref, lq2_ref, lk2_ref, subln_ref, o_ref, *, lam_init):
    t = q_ref.shape[0]
    past = kt_ref.shape[2]
    lam = _diff_lambda(lq1_ref, lk1_ref, lq2_ref, lk2_ref, lam_init)
    for hd in range(BRANCH_W // SLAB):
        cols = slice(hd * SLAB, (hd + 1) * SLAB)
        q2 = _split_pair(q_ref[:, cols])
        kt = kt_ref[2 * hd:2 * hd + 2].reshape(SLAB, past).astype(BF16)
        s_c = _dot(q2, kt)
        s_n = _dot_nt(q2, kn_ref[:, cols]) + nbias_ref[...]
        m = jnp.maximum(jnp.max(s_c, axis=-1, keepdims=True), jnp.max(s_n, axis=-1, keepdims=True))
        p_c = jnp.exp2(s_c - m)
        p_n = jnp.exp2(s_n - m)
        l = jnp.sum(p_c, axis=-1, keepdims=True) + jnp.sum(p_n, axis=-1, keepdims=True)
        o = _dot(p_c.astype(BF16), vc_ref[:, cols].astype(BF16)) + _dot(p_n.astype(BF16), vn_ref[:, cols])
        o = o * (1.0 / l)
        od = o[:t] - lam * o[t:]
        od = od * lax.rsqrt(jnp.mean(od * od, axis=-1, keepdims=True) + RMS_EPS) * subln_ref[...] * (1.0 - lam_init)
        o_ref[:, cols] = (od * sz_ref[:, cols].astype(F32)).astype(BF16)


def _diff_attn_sample(qb, cache_kt, cache_v, k_new, v_new, nbias, szb, lq1, lk1, lq2, lk2, subln_row, lam_init):
    b, t, _ = qb.shape
    n_new = k_new.shape[1]
    tok = pl.BlockSpec((None, t, BRANCH_W), lambda bi: (bi, 0, 0))
    new = pl.BlockSpec((None, n_new, BRANCH_W), lambda bi: (bi, 0, 0))
    whole = lambda a: pl.BlockSpec((None,) + a.shape[1:], lambda bi: (bi,) + (0,) * (a.ndim - 1))
    small = lambda a: pl.BlockSpec(a.shape, lambda bi: (0, 0))
    return pl.pallas_call(
        functools.partial(_diff_attn_sample_kernel, lam_init=lam_init),
        grid=(b,),
        in_specs=[tok, whole(cache_kt), whole(cache_v), new, new, small(nbias), tok,
                  small(lq1), small(lk1), small(lq2), small(lk2), small(subln_row)],
        out_specs=tok,
        out_shape=jax.ShapeDtypeStruct((b, t, BRANCH_W), BF16),
        compiler_params=pltpu.CompilerParams(dimension_semantics=("arbitrary",), vmem_limit_bytes=VMEM_LIMIT),
        name="diff_attn_sample",
    )(qb, cache_kt, cache_v, k_new, v_new, nbias, szb, lq1, lk1, lq2, lk2, subln_row)


def _rope_tables(pos):
    inv = 1.0 / (ROPE_THETA ** (jnp.arange(0, HALF, 2, dtype=F32) / HALF))
    ang = pos.astype(F32)[:, None] * inv[None, :]
    cos = jnp.cos(ang)
    sin = jnp.sin(ang)
    return jnp.tile(cos, (1, SLAB // (HALF // 2))), jnp.tile(jnp.concatenate([-sin, sin], axis=1), (1, SLAB // HALF))


def _bias_lookup(rb_ref, pair, dist, head_in_pair):
    idx = jnp.clip(dist, -REL_CLIP, REL_CLIP) + REL_CLIP

    def entry(k, acc):
        v = jnp.where(head_in_pair == 1, rb_ref[2 * pair + 1, k], rb_ref[2 * pair, k])
        return jnp.where(idx == k, v, acc)

    return LOG2E * lax.fori_loop(0, 2 * REL_CLIP + 1, entry, jnp.zeros(dist.shape, F32))


def _prompt_bias_kernel(rb_ref, o_ref):
    pair = pl.program_id(0)

    def rows(u0, n):
        u = u0 + lax.broadcasted_iota(jnp.int32, (n, 2 * Q_BLK), 0)
        c = lax.broadcasted_iota(jnp.int32, (n, 2 * Q_BLK), 1)
        ql = c & (Q_BLK - 1)
        visible = ((ql < CHUNK) & (u < A_PAST + CHUNK)) | ((ql >= CHUNK) & (u >= CHUNK))
        return ql + A_PAST - u, jnp.where(c >= Q_BLK, 1, 0), visible

    dist, head_in_pair, visible = rows(BAND_CONST, BAND - BAND_CONST)
    o_ref[BAND_CONST:BAND, :] = jnp.where(visible, _bias_lookup(rb_ref, pair, dist, head_in_pair), NEG_INF)
    _, head_in_pair, visible = rows(0, BAND_CONST)
    far = LOG2E * jnp.where(head_in_pair == 1, rb_ref[2 * pair + 1, 2 * REL_CLIP], rb_ref[2 * pair, 2 * REL_CLIP])
    o_ref[:BAND_CONST, :] = jnp.where(visible, far, NEG_INF)
    o_ref[BAND:, :] = jnp.full((o_ref.shape[0] - BAND, 2 * Q_BLK), NEG_INF, F32)


def _prompt_bias_table(rel_bias):
    n_pair = rel_bias.shape[0] // 2
    rows = BAND + A_PAST
    return pl.pallas_call(
        _prompt_bias_kernel,
        grid=(n_pair,),
        in_specs=[pl.BlockSpec(memory_space=pltpu.SMEM)],
        out_specs=pl.BlockSpec((None, rows, 2 * Q_BLK), lambda p: (p, 0, 0)),
        out_shape=jax.ShapeDtypeStruct((n_pair, rows, 2 * Q_BLK), F32),
        compiler_params=pltpu.CompilerParams(dimension_semantics=("arbitrary",)),
        name="prompt_bias",
    )(rel_bias)


def _sample_bias_kernel(rb_ref, o_ref, *, p_len, t):
    pair = pl.program_id(0)
    r = lax.broadcasted_iota(jnp.int32, o_ref.shape, 0)
    j = lax.broadcasted_iota(jnp.int32, o_ref.shape, 1)
    head_in_pair = jnp.where(r >= t, 1, 0)
    dist = p_len + r - t * head_in_pair - j
    o_ref[...] = jnp.where(j < p_len + t, _bias_lookup(rb_ref, pair, dist, head_in_pair), NEG_INF)


def _sample_bias_table(rel_bias, p_len, t, n_keys):
    n_pair = rel_bias.shape[0] // 2
    return pl.pallas_call(
        functools.partial(_sample_bias_kernel, p_len=p_len, t=t),
        grid=(n_pair,),
        in_specs=[pl.BlockSpec(memory_space=pltpu.SMEM)],
        out_specs=pl.BlockSpec((None, 2 * t, n_keys), lambda p: (p, 0, 0)),
        out_shape=jax.ShapeDtypeStruct((n_pair, 2 * t, n_keys), F32),
        compiler_params=pltpu.CompilerParams(dimension_semantics=("arbitrary",)),
        name="sample_bias",
    )(rel_bias)


def _round_up(n, m):
    return (n + m - 1) // m * m


def kernel(x_prompt, x_sample, cache_a_k, cache_a_v, cache_b_k, cache_b_v, cache_mem_k, cache_mem_v, mem_prompt, norm_in, w_in, rel_bias, lambda_q1, lambda_k1, lambda_q2, lambda_k2, subln, norm_mem, w_mem_kv, w_branch_a, w_branch_b, w_branch_m, w_out, norm_final):
    depth = w_in.shape[0]
    assert depth == 1, "kernels are written for the single-layer step"
    bsz, seq, d_model = x_prompt.shape
    dec_b, dec_t, _ = x_sample.shape
    past = cache_b_k.shape[2]
    a_cache = cache_a_k.shape[2]
    n_mem = mem_prompt.shape[1]
    keep = min(A_PAST, seq)
    lam_init = 0.8 - 0.6 * math.exp(-0.3 * 0)
    l = 0

    w_in_b = w_in[l].astype(BF16)
    w_mem_b = w_mem_kv[l].astype(BF16)
    wa, wb, wm, wo = (w[l].astype(BF16) for w in (w_branch_a, w_branch_b, w_branch_m, w_out))
    g_in = norm_in[l][None, :]
    g_mem = norm_mem[l][None, :]
    g_final = norm_final[None, :]
    lq1, lk1, lq2, lk2 = (v[l][None, :] for v in (lambda_q1, lambda_k1, lambda_q2, lambda_k2))

    cos_p, sin_p = _rope_tables(jnp.arange(seq))
    (qa, ka, va_t, ka_keep, va_keep, qb, kb, kb_f32, vb_t, vb_f32, qm, sza, szb, szm, sg) = _in_proj(
        x_prompt, g_in, w_in_b, cos_p, sin_p, keep=keep, transposed_v=True)
    ga = _chunk_attn_prompt(qa, ka, va_t, _prompt_bias_table(rel_bias[l]), sza)
    gb = _diff_attn_prompt(qb, kb, vb_t, szb, lq1, lk1, lq2, lk2, subln[l][:, None], lam_init)
    mk_f32, mv_f32, mk, mv = _mem_kv(mem_prompt, g_mem, w_mem_b)
    gm = _mem_attn(qm, mk, mv, szm)
    y_prompt = _merge(ga, gb, gm, sg, x_prompt, wa, wb, wm, wo, g_final)

    n_tok = dec_b * dec_t
    pos_s = past + jnp.tile(jnp.arange(dec_t), dec_b)
    cos_s, sin_s = _rope_tables(pos_s)
    xs = x_sample.reshape(1, n_tok, d_model)
    (qa_s, ka_s, va_s, ka_s32, va_s32, qb_s, kb_s, kb_s32, vb_s, vb_s32, qm_s, sza_s, szb_s, szm_s, sg_s) = _in_proj(
        xs, g_in, w_in_b, cos_s, sin_s, keep=n_tok, transposed_v=False)
    per_req = lambda a: a.reshape(dec_b, dec_t, a.shape[-1])

    n_keys = _round_up(a_cache + dec_t, SLAB)
    pad_a = ((0, 0), (0, n_keys - a_cache - dec_t), (0, 0))
    k_cat = jnp.pad(jnp.concatenate([cache_a_k[l].reshape(dec_b, a_cache, BRANCH_W).astype(BF16), per_req(ka_s)], axis=1), pad_a)
    v_cat = jnp.pad(jnp.concatenate([cache_a_v[l].reshape(dec_b, a_cache, BRANCH_W).astype(BF16), per_req(va_s)], axis=1), pad_a)
    ga_s = _chunk_attn_sample(per_req(qa_s), k_cat, v_cat, _sample_bias_table(rel_bias[l], a_cache, dec_t, n_keys),
                              per_req(sza_s))

    n_new = _round_up(dec_t, SLAB)
    pad_b = ((0, 0), (0, n_new - dec_t), (0, 0))
    nbias = jnp.where(jnp.arange(n_new) < dec_t, 0.0, NEG_INF).astype(F32)[None, :]
    gb_s = _diff_attn_sample(per_req(qb_s), jnp.transpose(cache_b_k[l], (0, 2, 3, 1)),
                             cache_b_v[l].reshape(dec_b, past, BRANCH_W),
                             jnp.pad(per_req(kb_s), pad_b), jnp.pad(per_req(vb_s), pad_b), nbias, per_req(szb_s),
                             lq1, lk1, lq2, lk2, subln[l][None, :], lam_init)
    gm_s = _mem_attn(per_req(qm_s), cache_mem_k[l].reshape(dec_b, n_mem, BRANCH_W),
                     cache_mem_v[l].reshape(dec_b, n_mem, BRANCH_W), per_req(szm_s))
    flat = lambda a: a.reshape(1, n_tok, a.shape[-1])
    y_sample = _merge(flat(ga_s), flat(gb_s), flat(gm_s), sg_s, xs, wa, wb, wm, wo, g_final).reshape(x_sample.shape)

    a_heads, a_dim = cache_a_k.shape[3:]
    bk_heads, bk_dim = cache_b_k.shape[3:]
    bv_heads, bv_dim = cache_b_v.shape[3:]
    m_heads, m_dim = cache_mem_k.shape[3:]
    return (y_prompt, y_sample,
            ka_keep.reshape(1, bsz, keep, a_heads, a_dim), va_keep.reshape(1, bsz, keep, a_heads, a_dim),
            kb_f32.reshape(1, bsz, seq, bk_heads, bk_dim), vb_f32.reshape(1, bsz, seq, bv_heads, bv_dim),
            mk_f32.reshape(1, bsz, n_mem, m_heads, m_dim), mv_f32.reshape(1, bsz, n_mem, m_heads, m_dim),
            ka_s32.reshape(1, dec_b, dec_t, a_heads, a_dim), va_s32.reshape(1, dec_b, dec_t, a_heads, a_dim),
            kb_s32.reshape(1, dec_b, dec_t, bk_heads, bk_dim), vb_s32.reshape(1, dec_b, dec_t, bv_heads, bv_dim))
```

```python
import functools
import math

import jax
import jax.numpy as jnp
from jax import lax
from jax.experimental import pallas as pl
from jax.experimental.pallas import tpu as pltpu

F32 = jnp.float32
BF16 = jnp.bfloat16

CHUNK = 64
CHUNK_SHIFT = 6
A_PAST = 8 * CHUNK
REL_CLIP = 128
BRANCH_W = 512
SLAB = 128
HALF = 64
ROPE_THETA = 10000.0
RMS_EPS = 1e-6
NEG_INF = -1e30

PROJ_ROWS = 512
VT_BLK = 256
VT_ROWS = SLAB + 16
KV_STEP = 512
Q_BLK = 128
BAND = A_PAST + Q_BLK
BAND_CONST = A_PAST - REL_CLIP
DQ_BLK = 512
LOG2E = math.log2(math.e)
PIPE_SLOTS = 3
COL_GROUP = 256
BODY_STEPS = PIPE_SLOTS
CHUNK_UNROLL = 32
assert DQ_BLK == KV_STEP

VMEM_LIMIT = 56 * 1024 * 1024


def _dot(a, b):
    return jnp.dot(a, b, preferred_element_type=F32)


def _dot_nt(a, b):
    return lax.dot_general(a, b, (((1,), (1,)), ((), ())), preferred_element_type=F32)


def _silu(z):
    return z * (1.0 / (1.0 + jnp.exp(-z)))


def _sigmoid(z):
    return 1.0 / (1.0 + jnp.exp(-z))


def _split_pair(q):
    lane = lax.broadcasted_iota(jnp.int32, q.shape, 1)
    zero = jnp.zeros_like(q)
    return jnp.concatenate([jnp.where(lane < HALF, q, zero), jnp.where(lane >= HALF, q, zero)], axis=0)


def _diff_lambda(lq1_ref, lk1_ref, lq2_ref, lk2_ref, lam_init):
    e1 = jnp.exp(jnp.sum(lq1_ref[...] * lk1_ref[...], axis=-1, keepdims=True))
    e2 = jnp.exp(jnp.sum(lq2_ref[...] * lk2_ref[...], axis=-1, keepdims=True))
    return e1 - e2 + lam_init


def _in_proj_kernel(x_ref, g_ref, w_ref, cos_ref, sin_ref,
                    qa_ref, ka_ref, va_ref, kaw_ref, vaw_ref,
                    qb_ref, kb_ref, kbf_ref, vb_ref, vbf_ref,
                    qm_ref, sza_ref, szb_ref, szm_ref, sg_ref, *, transposed_v, mem_scale):
    tm = x_ref.shape[0]
    x = x_ref[...]
    h = (x * lax.rsqrt(jnp.mean(x * x, axis=-1, keepdims=True) + RMS_EPS) * g_ref[...]).astype(BF16)

    def proj(idx):
        return _dot(h, w_ref[:, idx * BRANCH_W:(idx + 1) * BRANCH_W])

    cos = cos_ref[...]
    sin = sin_ref[...]
    lane = lax.broadcasted_iota(jnp.int32, (tm, SLAB), 1)
    first_half = (lane & (HALF // 2)) == 0

    def rope_slabs(acc):
        out = []
        for j in range(BRANCH_W // SLAB):
            y = acc[:, j * SLAB:(j + 1) * SLAB]
            partner = jnp.where(first_half, pltpu.roll(y, SLAB - HALF // 2, 1), pltpu.roll(y, HALF // 2, 1))
            out.append(y * cos + partner * sin)
        return out

    qa_ref[...] = (proj(0) * (LOG2E * HALF ** -0.5)).astype(BF16)
    ak = proj(1)
    ka_ref[...] = ak.astype(BF16)
    kaw_ref[...] = ak
    av = proj(2)
    vaw_ref[...] = av
    if transposed_v:
        for p in range(BRANCH_W // SLAB):
            for i in range(tm // SLAB):
                va_ref[p, i] = av[i * SLAB:(i + 1) * SLAB, p * SLAB:(p + 1) * SLAB].T.astype(BF16)
    else:
        va_ref[...] = av.astype(BF16)
    sza_ref[...] = _silu(proj(3)).astype(BF16)

    for j, y in enumerate(rope_slabs(proj(4))):
        qb_ref[:, j * SLAB:(j + 1) * SLAB] = (y * (LOG2E * HALF ** -0.5)).astype(BF16)
    for j, y in enumerate(rope_slabs(proj(5))):
        kbf_ref[:, j * SLAB:(j + 1) * SLAB] = y
        kb_ref[:, j * SLAB:(j + 1) * SLAB] = y.astype(BF16)
    bv = proj(6)
    vbf_ref[...] = bv.reshape(vbf_ref.shape)
    if transposed_v:
        for hd in range(BRANCH_W // SLAB):
            for i in range(tm // VT_BLK):
                vb_ref[hd, i, :SLAB, :] = bv[i * VT_BLK:(i + 1) * VT_BLK, hd * SLAB:(hd + 1) * SLAB].T.astype(BF16)
                vb_ref[hd, i, SLAB:, :] = jnp.ones((VT_ROWS - SLAB, VT_BLK), BF16)
    else:
        vb_ref[...] = bv.astype(BF16)
    szb_ref[...] = _silu(proj(7)).astype(BF16)

    qm_ref[...] = (proj(8) * mem_scale).astype(BF16)
    szm_ref[...] = _silu(proj(9)).astype(BF16)
    for j in range(sg_ref.shape[1] // BRANCH_W):
        sg_ref[:, j * BRANCH_W:(j + 1) * BRANCH_W] = _sigmoid(proj(10 + j)).astype(BF16)


def _in_proj(x, norm_g, w_bf16, cos_tab, sin_tab, *, keep, transposed_v):
    b, s, d = x.shape
    d_in = w_bf16.shape[1]
    tm = min(PROJ_ROWS, s)
    nt = s // tm
    first_kept = (s - keep) // tm
    n_gate = d_in - 10 * BRANCH_W

    row = lambda bi, j: (bi, j, 0)
    kept = lambda bi, j: (bi, jnp.maximum(j - first_kept, 0), 0)
    act = lambda dt: jax.ShapeDtypeStruct((b, s, BRANCH_W), dt)
    act_spec = pl.BlockSpec((None, tm, BRANCH_W), row)
    win = jax.ShapeDtypeStruct((b, keep, BRANCH_W), F32)
    win_spec = pl.BlockSpec((None, tm, BRANCH_W), kept)
    if transposed_v:
        va = jax.ShapeDtypeStruct((b, BRANCH_W // SLAB, s // SLAB, SLAB, SLAB), BF16)
        va_spec = pl.BlockSpec((None, BRANCH_W // SLAB, tm // SLAB, SLAB, SLAB), lambda bi, j: (bi, 0, j, 0, 0))
        vb = jax.ShapeDtypeStruct((b, BRANCH_W // SLAB, s // VT_BLK, VT_ROWS, VT_BLK), BF16)
        vb_spec = pl.BlockSpec((None, BRANCH_W // SLAB, tm // VT_BLK, VT_ROWS, VT_BLK), lambda bi, j: (bi, 0, j, 0, 0))
    else:
        va, va_spec, vb, vb_spec = act(BF16), act_spec, act(BF16), act_spec

    out_shape = [act(BF16), act(BF16), va, win, win,
                 act(BF16), act(BF16), act(F32), vb, jax.ShapeDtypeStruct((b, s, BRANCH_W // SLAB, SLAB), F32),
                 act(BF16), act(BF16), act(BF16), act(BF16),
                 jax.ShapeDtypeStruct((b, s, n_gate), BF16)]
    out_specs = [act_spec, act_spec, va_spec, win_spec, win_spec,
                 act_spec, act_spec, act_spec, vb_spec,
                 pl.BlockSpec((None, tm, BRANCH_W // SLAB, SLAB), lambda bi, j: (bi, j, 0, 0)),
                 act_spec, act_spec, act_spec, act_spec,
                 pl.BlockSpec((None, tm, n_gate), row)]
    return pl.pallas_call(
        functools.partial(_in_proj_kernel, transposed_v=transposed_v, mem_scale=LOG2E * SLAB ** -0.5),
        grid=(b, nt),
        in_specs=[pl.BlockSpec((None, tm, d), row),
                  pl.BlockSpec((1, d), lambda bi, j: (0, 0)),
                  pl.BlockSpec((d, d_in), lambda bi, j: (0, 0), pipeline_mode=pl.Buffered(1)),
                  pl.BlockSpec((tm, SLAB), lambda bi, j: (j, 0)),
                  pl.BlockSpec((tm, SLAB), lambda bi, j: (j, 0))],
        out_specs=out_specs,
        out_shape=out_shape,
        compiler_params=pltpu.CompilerParams(dimension_semantics=("arbitrary", "arbitrary"),
                                             vmem_limit_bytes=VMEM_LIMIT),
        name="in_proj",
    )(x, norm_g, w_bf16, cos_tab, sin_tab)


def _chunk_attn_prompt_kernel(q_ref, k_ref, vt_ref, bias_ref, sz_ref, o_ref):
    s_len = q_ref.shape[0]

    def q_block(g, carry):
        q0 = pl.multiple_of(g * Q_BLK, Q_BLK)
        q2 = _split_pair(q_ref[pl.ds(q0, Q_BLK), :])
        start = g * Q_BLK - A_PAST
        start_c = jnp.maximum(start, 0)
        off = pl.multiple_of(start_c - start, SLAB)
        start_c = pl.multiple_of(start_c, SLAB)
        kb = k_ref[pl.ds(start_c, BAND), :]
        s = _dot_nt(kb, q2) + bias_ref[pl.ds(off, BAND), :]
        m = jnp.max(s, axis=0, keepdims=True)
        p = jnp.exp2(s - m)
        l = jnp.sum(p, axis=0, keepdims=True)
        pb = p.astype(BF16)
        blk0 = start_c // SLAB
        acc = jnp.zeros((SLAB, 2 * Q_BLK), F32)
        for i in range(BAND // SLAB):
            acc = acc + _dot(vt_ref[blk0 + i], pb[i * SLAB:(i + 1) * SLAB, :])
        acc = acc * (1.0 / l)
        o_t = jnp.concatenate([acc[:HALF, :Q_BLK], acc[HALF:, Q_BLK:]], axis=0)
        o = o_t.T * sz_ref[pl.ds(q0, Q_BLK), :].astype(F32)
        o_ref[pl.ds(q0, Q_BLK), :] = o.astype(BF16)
        return carry

    lax.fori_loop(0, s_len // Q_BLK, q_block, 0, unroll=CHUNK_UNROLL)


def _chunk_attn_prompt(qa, ka, va_t, bias_t, sza):
    b, s, _ = qa.shape
    n_pair = BRANCH_W // SLAB
    slab = pl.BlockSpec((None, s, SLAB), lambda bi, p: (bi, 0, p))
    return pl.pallas_call(
        _chunk_attn_prompt_kernel,
        grid=(b, n_pair),
        in_specs=[slab, slab,
                  pl.BlockSpec((None, None, s // SLAB, SLAB, SLAB), lambda bi, p: (bi, p, 0, 0, 0)),
                  pl.BlockSpec((None, BAND + A_PAST, 2 * Q_BLK), lambda bi, p: (p, 0, 0)),
                  slab],
        out_specs=slab,
        out_shape=jax.ShapeDtypeStruct((b, s, BRANCH_W), BF16),
        compiler_params=pltpu.CompilerParams(dimension_semantics=("arbitrary", "arbitrary"),
                                             vmem_limit_bytes=VMEM_LIMIT),
        name="chunk_attn_prompt",
    )(qa, ka, va_t, bias_t, sza)


def _diff_attn_schedule(n_q):
    steps = [(qi, qi) for qi in range(n_q)]
    steps += [(qi, kv) for kv in range(n_q) for qi in range(kv + 1, n_q)]
    return steps


def _diff_attn_prompt_kernel(sched_ref, q_ref, k_ref, vt_ref, sz_ref, lq1_ref, lk1_ref, lq2_ref, lk2_ref, subln_ref,
                             o_ref, q2_ref, kz_ref, m_ref, acc_ref, s_ref, p_ref, *, lam_init):
    n_q = q_ref.shape[0] // DQ_BLK
    n_steps = sched_ref.shape[1]
    sub_blocks = KV_STEP // VT_BLK

    @pl.when((pl.program_id(0) == 0) & (pl.program_id(1) == 0))
    def _():
        k_chunk = jnp.right_shift(lax.broadcasted_iota(jnp.int32, kz_ref.shape[1:], 0), CHUNK_SHIFT)
        chunk_id = lax.broadcasted_iota(jnp.int32, kz_ref.shape[1:], 1)
        kz_ref[0] = jnp.zeros(kz_ref.shape[1:], BF16)
        kz_ref[1] = jnp.where(k_chunk == chunk_id, 1.0, 0.0).astype(BF16)
        q_col = lax.broadcasted_iota(jnp.int32, (SLAB, 2 * DQ_BLK), 1)
        q_chunk = jnp.right_shift(q_col & (DQ_BLK - 1), CHUNK_SHIFT)
        hidden = lax.broadcasted_iota(jnp.int32, (SLAB, 2 * DQ_BLK), 0) > q_chunk
        for qi in range(n_q):
            q2_ref[qi, SLAB:, :] = jnp.where(hidden, NEG_INF, 0.0).astype(BF16)

    dim = lax.broadcasted_iota(jnp.int32, (SLAB, DQ_BLK), 0)
    for qi in range(n_q):
        q_t = q_ref[qi * DQ_BLK:(qi + 1) * DQ_BLK, :].astype(F32).T.astype(BF16)
        q2_ref[qi, :SLAB, :DQ_BLK] = jnp.where(dim < HALF, q_t, jnp.zeros_like(q_t))
        q2_ref[qi, :SLAB, DQ_BLK:] = jnp.where(dim >= HALF, q_t, jnp.zeros_like(q_t))
    m_ref[...] = jnp.full(m_ref.shape, NEG_INF, F32)
    acc_ref[...] = jnp.zeros(acc_ref.shape, F32)
    p_ref[PIPE_SLOTS - 1] = jnp.zeros(p_ref.shape[1:], BF16)

    n_groups = 2 * DQ_BLK // COL_GROUP

    def cols(g):
        return slice(g * COL_GROUP, (g + 1) * COL_GROUP)

    def scores(t, slot, g):
        t = jnp.minimum(t, n_steps - 1)
        qi = sched_ref[0, t]
        kv = sched_ref[1, t]
        k0 = pl.multiple_of(kv * KV_STEP, KV_STEP)
        keys = jnp.concatenate([k_ref[pl.ds(k0, KV_STEP), :], kz_ref[jnp.where(qi == kv, 1, 0)]], axis=1)
        s_ref[slot, :, cols(g)] = _dot(keys, q2_ref[qi, :, cols(g)])

    def softmax(t, slot, g):
        qi = sched_ref[0, t]
        s = s_ref[slot, :, cols(g)]
        m_old = m_ref[qi, :, cols(g)]
        m = jnp.maximum(m_old, jnp.max(s, axis=0, keepdims=True))
        m_ref[qi, :, cols(g)] = m
        p_ref[slot, :, cols(g)] = jnp.exp2(s - m).astype(BF16)
        return jnp.exp2(m_old - m)

    def values(t, slot, g, alpha):
        t = jnp.maximum(t, 0)
        qi = sched_ref[0, t]
        kv = sched_ref[1, t]
        pv = _dot(vt_ref[sub_blocks * kv], p_ref[slot, :VT_BLK, cols(g)])
        for i in range(1, sub_blocks):
            pv = pv + _dot(vt_ref[sub_blocks * kv + i], p_ref[slot, i * VT_BLK:(i + 1) * VT_BLK, cols(g)])
        acc_ref[qi, :, cols(g)] = alpha * acc_ref[qi, :, cols(g)] + pv

    def ring_steps(j, alphas):
        for u in range(BODY_STEPS):
            t = BODY_STEPS * j + u
            new_alphas = []
            for g in range(n_groups):
                scores(t + 2, (u + 2) % PIPE_SLOTS, g)
                new_alphas.append(softmax(t, u % PIPE_SLOTS, g))
                values(t - 1, (u - 1) % PIPE_SLOTS, g, alphas[g])
            alphas = tuple(new_alphas)
        return alphas

    for g in range(n_groups):
        scores(0, 0, g)
        scores(1, 1, g)
    alphas = lax.fori_loop(0, n_steps // BODY_STEPS, ring_steps,
                           tuple(jnp.ones((1, COL_GROUP), F32) for _ in range(n_groups)))
    for g in range(n_groups):
        values(n_steps - 1, (n_steps - 1) % PIPE_SLOTS, g, alphas[g])

    lam = _diff_lambda(lq1_ref, lk1_ref, lq2_ref, lk2_ref, lam_init)

    def finish(qi, carry):
        q0 = pl.multiple_of(qi * DQ_BLK, DQ_BLK)
        acc = acc_ref[qi, :SLAB, :] * (1.0 / acc_ref[qi, SLAB:SLAB + 1, :])
        o_t = acc[:, :DQ_BLK] - lam * acc[:, DQ_BLK:]
        o_t = o_t * lax.rsqrt(jnp.mean(o_t * o_t, axis=0, keepdims=True) + RMS_EPS)
        o_t = o_t * subln_ref[...] * (1.0 - lam_init)
        o = o_t.T * sz_ref[pl.ds(q0, DQ_BLK), :].astype(F32)
        o_ref[pl.ds(q0, DQ_BLK), :] = o.astype(BF16)
        return carry

    lax.fori_loop(0, n_q, finish, 0, unroll=True)


def _diff_attn_prompt(qb, kb, vb_t, szb, lq1, lk1, lq2, lk2, subln_col, lam_init):
    b, s, _ = qb.shape
    n_head = BRANCH_W // SLAB
    n_q = s // DQ_BLK
    steps = _diff_attn_schedule(n_q)
    assert len(steps) % BODY_STEPS == 0, "one loop body covers BODY_STEPS steps"
    assert KV_STEP // CHUNK <= SLAB, "chunk ids of a diagonal block must fit the spare contraction lanes"
    sched = jnp.asarray([[qi for qi, _ in steps], [kv for _, kv in steps]], jnp.int32)
    slab = pl.BlockSpec((None, s, SLAB), lambda bi, hd: (bi, 0, hd))
    small = lambda shape: pl.BlockSpec(shape, lambda bi, hd: (0, 0))
    return pl.pallas_call(
        functools.partial(_diff_attn_prompt_kernel, lam_init=lam_init),
        grid=(b, n_head),
        in_specs=[pl.BlockSpec(memory_space=pltpu.SMEM), slab, slab,
                  pl.BlockSpec((None, None, s // VT_BLK, VT_ROWS, VT_BLK), lambda bi, hd: (bi, hd, 0, 0, 0)),
                  slab,
                  small(lq1.shape), small(lk1.shape), small(lq2.shape), small(lk2.shape), small(subln_col.shape)],
        out_specs=slab,
        out_shape=jax.ShapeDtypeStruct((b, s, BRANCH_W), BF16),
        scratch_shapes=[pltpu.VMEM((n_q, 2 * SLAB, 2 * DQ_BLK), BF16),
                        pltpu.VMEM((2, KV_STEP, SLAB), BF16),
                        pltpu.VMEM((n_q, 1, 2 * DQ_BLK), F32),
                        pltpu.VMEM((n_q, VT_ROWS, 2 * DQ_BLK), F32),
                        pltpu.VMEM((PIPE_SLOTS, KV_STEP, 2 * DQ_BLK), F32),
                        pltpu.VMEM((PIPE_SLOTS, KV_STEP, 2 * DQ_BLK), BF16)],
        compiler_params=pltpu.CompilerParams(dimension_semantics=("arbitrary", "arbitrary"),
                                             vmem_limit_bytes=VMEM_LIMIT),
        name="diff_attn_prompt",
    )(sched, qb, kb, vb_t, szb, lq1, lk1, lq2, lk2, subln_col)


def _mem_kv_kernel(mem_ref, g_ref, w_ref, kf_ref, vf_ref, kb_ref, vb_ref):
    x = mem_ref[...]
    h = (x * lax.rsqrt(jnp.mean(x * x, axis=-1, keepdims=True) + RMS_EPS) * g_ref[...]).astype(BF16)
    k = _dot(h, w_ref[:, :BRANCH_W])
    v = _dot(h, w_ref[:, BRANCH_W:])
    kf_ref[...] = k.reshape(kf_ref.shape)
    vf_ref[...] = v.reshape(vf_ref.shape)
    kb_ref[...] = k.astype(BF16)
    vb_ref[...] = v.astype(BF16)


def _mem_kv(mem, norm_g, w_bf16):
    b, n, d = mem.shape
    blk = lambda width: pl.BlockSpec((None, n, width), lambda bi: (bi, 0, 0))
    out = lambda dt: jax.ShapeDtypeStruct((b, n, BRANCH_W), dt)
    heads = jax.ShapeDtypeStruct((b, n, BRANCH_W // SLAB, SLAB), F32)
    heads_blk = pl.BlockSpec((None, n, BRANCH_W // SLAB, SLAB), lambda bi: (bi, 0, 0, 0))
    return pl.pallas_call(
        _mem_kv_kernel,
        grid=(b,),
        in_specs=[blk(d), pl.BlockSpec((1, d), lambda bi: (0, 0)),
                  pl.BlockSpec((d, 2 * BRANCH_W), lambda bi: (0, 0))],
        out_specs=[heads_blk, heads_blk, blk(BRANCH_W), blk(BRANCH_W)],
        out_shape=[heads, heads, out(BF16), out(BF16)],
        compiler_params=pltpu.CompilerParams(dimension_semantics=("arbitrary",)),
        name="mem_kv",
    )(mem, norm_g, w_bf16)


def _mem_attn_kernel(q_ref, k_ref, v_ref, sz_ref, o_ref):
    n_mem = k_ref.shape[0]
    k_all = k_ref[...].reshape(n_mem, BRANCH_W).astype(BF16)
    v_all = v_ref[...].reshape(n_mem, BRANCH_W).astype(BF16)
    for hd in range(BRANCH_W // SLAB):
        cols = slice(hd * SLAB, (hd + 1) * SLAB)
        s = _dot_nt(q_ref[:, cols], k_all[:, cols])
        m = jnp.max(s, axis=-1, keepdims=True)
        p = jnp.exp2(s - m).astype(BF16)
        v = v_all[:, cols]
        ol = _dot(p, jnp.concatenate([v, jnp.ones_like(v)], axis=1))
        o = ol[:, :SLAB] / ol[:, SLAB:]
        o_ref[:, cols] = (o * sz_ref[:, cols].astype(F32)).astype(BF16)


def _mem_attn(qm, mk, mv, szm):
    b, s, _ = qm.shape
    tq = min(PROJ_ROWS, s)
    row = pl.BlockSpec((None, tq, BRANCH_W), lambda bi, j: (bi, j, 0))
    mem =pl.BlockSpec((None,) + mk.shape[1:], lambda bi, j: (bi,) + (0,) * (mk.ndim - 1))
    return pl.pallas_call(
        _mem_attn_kernel,
        grid=(b, s // tq),
        in_specs=[row, mem, mem, row],
        out_specs=row,
        out_shape=jax.ShapeDtypeStruct((b, s, BRANCH_W), BF16),
        compiler_params=pltpu.CompilerParams(dimension_semantics=("arbitrary", "arbitrary")),
        name="mem_attn",
    )(qm, mk, mv, szm)


def _merge_kernel(ga_ref, gb_ref, gm_ref, sg_ref, x_ref, wa_ref, wb_ref, wm_ref, wo_ref, nf_ref, y_ref):
    d = x_ref.shape[1]
    mixed = sg_ref[:, :d].astype(F32) * _dot(ga_ref[...], wa_ref[...])
    mixed = mixed + sg_ref[:, d:2 * d].astype(F32) * _dot(gb_ref[...], wb_ref[...])
    mixed = mixed + sg_ref[:, 2 * d:].astype(F32) * _dot(gm_ref[...], wm_ref[...])
    hsum = x_ref[...] + _dot(mixed.astype(BF16), wo_ref[...])
    y_ref[...] = hsum * lax.rsqrt(jnp.mean(hsum * hsum, axis=-1, keepdims=True) + RMS_EPS) * nf_ref[...]


def _merge(ga, gb, gm, sg, x, wa, wb, wm, wo, norm_final):
    b, s, d = x.shape
    tm = min(PROJ_ROWS, s)
    row = lambda width: pl.BlockSpec((None, tm, width), lambda bi, j: (bi, j, 0))
    full = lambda a: pl.BlockSpec(a.shape, lambda bi, j: (0, 0))
    return pl.pallas_call(
        _merge_kernel,
        grid=(b, s // tm),
        in_specs=[row(BRANCH_W), row(BRANCH_W), row(BRANCH_W), row(sg.shape[2]), row(d),
                  full(wa), full(wb), full(wm), full(wo), full(norm_final)],
        out_specs=row(d),
        out_shape=jax.ShapeDtypeStruct((b, s, d), F32),
        compiler_params=pltpu.CompilerParams(dimension_semantics=("arbitrary", "arbitrary"),
                                             vmem_limit_bytes=VMEM_LIMIT),
        name="merge",
    )(ga, gb, gm, sg, x, wa, wb, wm, wo, norm_final)


def _chunk_attn_sample_kernel(q_ref, kt_ref, vt_ref, kn_ref, vn_ref, bias_ref, sz_ref, o_ref):
    t = q_ref.shape[0]
    past = kt_ref.shape[2]
    lane = lax.broadcasted_iota(jnp.int32, (t, SLAB), 1)
    for p in range(BRANCH_W // SLAB):
        cols = slice(p * SLAB, (p + 1) * SLAB)
        q2 = _split_pair(q_ref[:, cols])
        kt = kt_ref[2 * p:2 * p + 2].reshape(SLAB, past).astype(BF16)
        vt = vt_ref[2 * p:2 * p + 2].reshape(SLAB, past).astype(BF16)
        s_c = _dot(q2, kt) + bias_ref[p, :, :past]
        s_n = _dot_nt(q2, kn_ref[:, cols]) + bias_ref[p, :, past:]
        m = jnp.maximum(jnp.max(s_c, axis=-1, keepdims=True), jnp.max(s_n, axis=-1, keepdims=True))
        p_c = jnp.exp2(s_c - m)
        p_n = jnp.exp2(s_n - m)
        l = jnp.sum(p_c, axis=-1, keepdims=True) + jnp.sum(p_n, axis=-1, keepdims=True)
        o = (_dot_nt(p_c.astype(BF16), vt) + _dot(p_n.astype(BF16), vn_ref[:, cols])) * (1.0 / l)
        o = jnp.where(lane < HALF, o[:t], o[t:])
        o_ref[:, cols] = (o * sz_ref[:, cols].astype(F32)).astype(BF16)


def _chunk_attn_sample(qa, cache_kt, cache_vt, k_new, v_new, bias, sza):
    b, t, _ = qa.shape
    n_new = k_new.shape[1]
    row = pl.BlockSpec((None, t, BRANCH_W), lambda bi: (bi, 0, 0))
    new = pl.BlockSpec((None, n_new, BRANCH_W), lambda bi: (bi, 0, 0))
    cache = pl.BlockSpec((None,) + cache_kt.shape[1:], lambda bi: (bi, 0, 0, 0))
    return pl.pallas_call(
        _chunk_attn_sample_kernel,
        grid=(b,),
        in_specs=[row, cache, cache, new, new, pl.BlockSpec(bias.shape, lambda bi: (0, 0, 0)), row],
        out_specs=row,
        out_shape=jax.ShapeDtypeStruct((b, t, BRANCH_W), BF16),
        compiler_params=pltpu.CompilerParams(dimension_semantics=("arbitrary",)),
        name="chunk_attn_sample",
    )(qa, cache_kt, cache_vt, k_new, v_new, bias, sza)


def _diff_attn_sample_kernel(q_ref, kt_ref, vc_ref, kn_ref, vn_ref, nbias_ref, sz_ref,
                             lq1_ref, lk1_ref, lq2_ref, lk2_ref, subln_ref, o_ref, *, lam_init):
    t = q_ref.shape[0]
    past = kt_ref.shape[2]
    lam = _diff_lambda(lq1_ref, lk1_ref, lq2_ref, lk2_ref, lam_init)
    v_cache = vc_ref[...].reshape(past, BRANCH_W).astype(BF16)
    for hd in range(BRANCH_W // SLAB):
        cols = slice(hd * SLAB, (hd + 1) * SLAB)
        q2 = _split_pair(q_ref[:, cols])
        kt = kt_ref[2 * hd:2 * hd + 2].reshape(SLAB, past).astype(BF16)
        s_c = _dot(q2, kt)
        s_n = _dot_nt(q2, kn_ref[:, cols]) + nbias_ref[...]
        m = jnp.maximum(jnp.max(s_c, axis=-1, keepdims=True), jnp.max(s_n, axis=-1, keepdims=True))
        p_c = jnp.exp2(s_c - m)
        p_n = jnp.exp2(s_n - m)
        l = jnp.sum(p_c, axis=-1, keepdims=True) + jnp.sum(p_n, axis=-1, keepdims=True)
        o = _dot(p_c.astype(BF16), v_cache[:, cols]) + _dot(p_n.astype(BF16), vn_ref[:, cols])
        o = o * (1.0 / l)
        od = o[:t] - lam * o[t:]
        od = od * lax.rsqrt(jnp.mean(od * od, axis=-1, keepdims=True) + RMS_EPS) * subln_ref[...] * (1.0 - lam_init)
        o_ref[:, cols] = (od * sz_ref[:, cols].astype(F32)).astype(BF16)


def _diff_attn_sample(qb, cache_kt, cache_v, k_new, v_new, nbias, szb, lq1, lk1, lq2, lk2, subln_row, lam_init):
    b, t, _ = qb.shape
    n_new = k_new.shape[1]
    tok = pl.BlockSpec((None, t, BRANCH_W), lambda bi: (bi, 0, 0))
    new = pl.BlockSpec((None, n_new, BRANCH_W), lambda bi: (bi, 0, 0))
    whole = lambda a: pl.BlockSpec((None,) + a.shape[1:], lambda bi: (bi,) + (0,) * (a.ndim - 1))
    small = lambda a: pl.BlockSpec(a.shape, lambda bi: (0, 0))
    return pl.pallas_call(
        functools.partial(_diff_attn_sample_kernel, lam_init=lam_init),
        grid=(b,),
        in_specs=[tok, whole(cache_kt), whole(cache_v), new, new, small(nbias), tok,
                  small(lq1), small(lk1), small(lq2), small(lk2), small(subln_row)],
        out_specs=tok,
        out_shape=jax.ShapeDtypeStruct((b, t, BRANCH_W), BF16),
        compiler_params=pltpu.CompilerParams(dimension_semantics=("arbitrary",), vmem_limit_bytes=VMEM_LIMIT),
        name="diff_attn_sample",
    )(qb, cache_kt, cache_v, k_new, v_new, nbias, szb, lq1, lk1, lq2, lk2, subln_row)


def _rope_tables(pos):
    inv = 1.0 / (ROPE_THETA ** (jnp.arange(0, HALF, 2, dtype=F32) / HALF))
    ang = pos.astype(F32)[:, None] * inv[None, :]
    cos = jnp.cos(ang)
    sin = jnp.sin(ang)
    return jnp.tile(cos, (1, SLAB // (HALF // 2))), jnp.tile(jnp.concatenate([-sin, sin], axis=1), (1, SLAB // HALF))


def _bias_lookup(rb_ref, pair, dist, head_in_pair):
    idx = jnp.clip(dist, -REL_CLIP, REL_CLIP) + REL_CLIP

    def entry(k, acc):
        v = jnp.where(head_in_pair == 1, rb_ref[2 * pair + 1, k], rb_ref[2 * pair, k])
        return jnp.where(idx == k, v, acc)

    return LOG2E * lax.fori_loop(0, 2 * REL_CLIP + 1, entry, jnp.zeros(dist.shape, F32))


def _prompt_bias_kernel(rb_ref, o_ref):
    pair = pl.program_id(0)

    def rows(u0, n):
        u = u0 + lax.broadcasted_iota(jnp.int32, (n, 2 * Q_BLK), 0)
        c = lax.broadcasted_iota(jnp.int32, (n, 2 * Q_BLK), 1)
        ql = c & (Q_BLK - 1)
        visible = ((ql < CHUNK) & (u < A_PAST + CHUNK)) | ((ql >= CHUNK) & (u >= CHUNK))
        return ql + A_PAST - u, jnp.where(c >= Q_BLK, 1, 0), visible

    dist, head_in_pair, visible = rows(BAND_CONST, BAND - BAND_CONST)
    o_ref[BAND_CONST:BAND, :] = jnp.where(visible, _bias_lookup(rb_ref, pair, dist, head_in_pair), NEG_INF)
    _, head_in_pair, visible = rows(0, BAND_CONST)
    far = LOG2E * jnp.where(head_in_pair == 1, rb_ref[2 * pair + 1, 2 * REL_CLIP], rb_ref[2 * pair, 2 * REL_CLIP])
    o_ref[:BAND_CONST, :] = jnp.where(visible, far, NEG_INF)
    o_ref[BAND:, :] = jnp.full((o_ref.shape[0] - BAND, 2 * Q_BLK), NEG_INF, F32)


def _prompt_bias_table(rel_bias):
    n_pair = rel_bias.shape[0] // 2
    rows = BAND + A_PAST
    return pl.pallas_call(
        _prompt_bias_kernel,
        grid=(n_pair,),
        in_specs=[pl.BlockSpec(memory_space=pltpu.SMEM)],
        out_specs=pl.BlockSpec((None, rows, 2 * Q_BLK), lambda p: (p, 0, 0)),
        out_shape=jax.ShapeDtypeStruct((n_pair, rows, 2 * Q_BLK), F32),
        compiler_params=pltpu.CompilerParams(dimension_semantics=("arbitrary",)),
        name="prompt_bias",
    )(rel_bias)


def _sample_bias_kernel(rb_ref, o_ref, *, p_len, t):
    pair = pl.program_id(0)
    r = lax.broadcasted_iota(jnp.int32, o_ref.shape, 0)
    j = lax.broadcasted_iota(jnp.int32, o_ref.shape, 1)
    head_in_pair = jnp.where(r >= t, 1, 0)
    dist = p_len + r - t * head_in_pair - j
    o_ref[...] = jnp.where(j < p_len + t, _bias_lookup(rb_ref, pair, dist, head_in_pair), NEG_INF)


def _sample_bias_table(rel_bias, p_len, t, n_keys):
    n_pair = rel_bias.shape[0] // 2
    return pl.pallas_call(
        functools.partial(_sample_bias_kernel, p_len=p_len, t=t),
        grid=(n_pair,),
        in_specs=[pl.BlockSpec(memory_space=pltpu.SMEM)],
        out_specs=pl.BlockSpec((None, 2 * t, n_keys), lambda p: (p, 0, 0)),
        out_shape=jax.ShapeDtypeStruct((n_pair, 2 * t, n_keys), F32),
        compiler_params=pltpu.CompilerParams(dimension_semantics=("arbitrary",)),
        name="sample_bias",
    )(rel_bias)


def _round_up(n, m):
    return (n + m - 1) // m * m


def kernel(x_prompt, x_sample, cache_a_k, cache_a_v, cache_b_k, cache_b_v, cache_mem_k, cache_mem_v, mem_prompt, norm_in, w_in, rel_bias, lambda_q1, lambda_k1, lambda_q2, lambda_k2, subln, norm_mem, w_mem_kv, w_branch_a, w_branch_b, w_branch_m, w_out, norm_final):
    depth = w_in.shape[0]
    assert depth == 1, "kernels are written for the single-layer step"
    bsz, seq, d_model = x_prompt.shape
    dec_b, dec_t, _ = x_sample.shape
    past = cache_b_k.shape[2]
    a_cache = cache_a_k.shape[2]
    n_mem = mem_prompt.shape[1]
    keep = min(A_PAST, seq)
    lam_init = 0.8 - 0.6 * math.exp(-0.3 * 0)
    l = 0

    w_in_b = w_in[l].astype(BF16)
    w_mem_b = w_mem_kv[l].astype(BF16)
    wa, wb, wm, wo = (w[l].astype(BF16) for w in (w_branch_a, w_branch_b, w_branch_m, w_out))
    g_in = norm_in[l][None, :]
    g_mem = norm_mem[l][None, :]
    g_final = norm_final[None, :]
    lq1, lk1, lq2, lk2 = (v[l][None, :] for v in (lambda_q1, lambda_k1, lambda_q2, lambda_k2))

    cos_p, sin_p = _rope_tables(jnp.arange(seq))
    (qa, ka, va_t, ka_keep, va_keep, qb, kb, kb_f32, vb_t, vb_f32, qm, sza, szb, szm, sg) = _in_proj(
        x_prompt, g_in, w_in_b, cos_p, sin_p, keep=keep, transposed_v=True)
    ga = _chunk_attn_prompt(qa, ka, va_t, _prompt_bias_table(rel_bias[l]), sza)
    gb = _diff_attn_prompt(qb, kb, vb_t, szb, lq1, lk1, lq2, lk2, subln[l][:, None], lam_init)
    mk_f32, mv_f32, mk, mv = _mem_kv(mem_prompt, g_mem, w_mem_b)
    gm = _mem_attn(qm, mk, mv, szm)
    y_prompt = _merge(ga, gb, gm, sg, x_prompt, wa, wb, wm, wo, g_final)

    n_tok = dec_b * dec_t
    pos_s = past + jnp.tile(jnp.arange(dec_t), dec_b)
    cos_s, sin_s = _rope_tables(pos_s)
    xs = x_sample.reshape(1, n_tok, d_model)
    (qa_s, ka_s, va_s, ka_s32, va_s32, qb_s, kb_s, kb_s32, vb_s, vb_s32, qm_s, sza_s, szb_s, szm_s, sg_s) = _in_proj(
        xs, g_in, w_in_b, cos_s, sin_s, keep=n_tok, transposed_v=False)
    per_req = lambda a: a.reshape(dec_b, dec_t, a.shape[-1])

    n_new = _round_up(dec_t, SLAB)
    pad_new = lambda a: jnp.pad(per_req(a), ((0, 0), (0, n_new - dec_t), (0, 0)))
    dims_major = lambda cache: jnp.transpose(cache, (0, 2, 3, 1))
    ga_s = _chunk_attn_sample(per_req(qa_s), dims_major(cache_a_k[l]), dims_major(cache_a_v[l]),
                              pad_new(ka_s), pad_new(va_s),
                              _sample_bias_table(rel_bias[l], a_cache, dec_t, a_cache + n_new), per_req(sza_s))

    nbias = jnp.where(jnp.arange(n_new) < dec_t, 0.0, NEG_INF).astype(F32)[None, :]
    gb_s = _diff_attn_sample(per_req(qb_s), dims_major(cache_b_k[l]), cache_b_v[l],
                             pad_new(kb_s), pad_new(vb_s), nbias, per_req(szb_s),
                             lq1, lk1, lq2, lk2, subln[l][None, :], lam_init)
    gm_s = _mem_attn(per_req(qm_s), cache_mem_k[l], cache_mem_v[l], per_req(szm_s))
    flat = lambda a: a.reshape(1, n_tok, a.shape[-1])
    y_sample = _merge(flat(ga_s), flat(gb_s), flat(gm_s), sg_s, xs, wa, wb, wm, wo, g_final).reshape(x_sample.shape)

    a_heads, a_dim = cache_a_k.shape[3:]
    bk_heads, bk_dim = cache_b_k.shape[3:]
    bv_heads, bv_dim = cache_b_v.shape[3:]
    m_heads, m_dim = cache_mem_k.shape[3:]
    return (y_prompt, y_sample,
            ka_keep.reshape(1, bsz, keep, a_heads, a_dim), va_keep.reshape(1, bsz, keep, a_heads, a_dim),
            kb_f32.reshape(1, bsz, seq, bk_heads, bk_dim), vb_f32.reshape(1, bsz, seq, bv_heads, bv_dim),
            mk_f32.reshape(1, bsz, n_mem, m_heads, m_dim), mv_f32.reshape(1, bsz, n_mem, m_heads, m_dim),
            ka_s32.reshape(1, dec_b, dec_t, a_heads, a_dim), va_s32.reshape(1, dec_b, dec_t, a_heads, a_dim),
            kb_s32.reshape(1, dec_b, dec_t, bk_heads, bk_dim), vb_s32.reshape(1, dec_b, dec_t, bv_heads, bv_dim))
```

```python
import functools
import math

import jax
import jax.numpy as jnp
from jax import lax
from jax.experimental import pallas as pl
from jax.experimental.pallas import tpu as pltpu

F32 = jnp.float32
BF16 = jnp.bfloat16

CHUNK = 64
CHUNK_SHIFT = 6
A_PAST = 8 * CHUNK
REL_CLIP = 128
BRANCH_W = 512
SLAB = 128
HALF = 64
ROPE_THETA = 10000.0
RMS_EPS = 1e-6
NEG_INF = -1e30

PROJ_ROWS = 512
MERGE_ROWS = 1024
MEM_ROWS = 4096
VT_BLK = 256
VT_ROWS = SLAB + 16
KV_STEP = 512
Q_BLK = 128
BAND = A_PAST + Q_BLK
BAND_CONST = A_PAST - REL_CLIP
DQ_BLK = 512
LOG2E = math.log2(math.e)
PIPE_SLOTS = 3
COL_GROUP = 256
BODY_STEPS = PIPE_SLOTS
assert DQ_BLK == KV_STEP

VMEM_LIMIT = 56 * 1024 * 1024


def _dot(a, b):
    return jnp.dot(a, b, preferred_element_type=F32)


def _dot_nt(a, b):
    return lax.dot_general(a, b, (((1,), (1,)), ((), ())), preferred_element_type=F32)


def _silu(z):
    return z * (1.0 / (1.0 + jnp.exp(-z)))


def _sigmoid(z):
    return 1.0 / (1.0 + jnp.exp(-z))


def _split_pair(q):
    lane = lax.broadcasted_iota(jnp.int32, q.shape, 1)
    zero = jnp.zeros_like(q)
    return jnp.concatenate([jnp.where(lane < HALF, q, zero), jnp.where(lane >= HALF, q, zero)], axis=0)


def _diff_lambda(lq1_ref, lk1_ref, lq2_ref, lk2_ref, lam_init):
    e1 = jnp.exp(jnp.sum(lq1_ref[...] * lk1_ref[...], axis=-1, keepdims=True))
    e2 = jnp.exp(jnp.sum(lq2_ref[...] * lk2_ref[...], axis=-1, keepdims=True))
    return e1 - e2 + lam_init


def _in_proj_kernel(x_ref, g_ref, w_ref, cos_ref, sin_ref,
                    qa_ref, ka_ref, va_ref, kaw_ref, vaw_ref,
                    qb_ref, kb_ref, kbf_ref, vb_ref, vbf_ref,
                    qm_ref, sza_ref, szb_ref, szm_ref, sg_ref, *, transposed_v, mem_scale):
    tm = x_ref.shape[0]
    x = x_ref[...]
    h = (x * lax.rsqrt(jnp.mean(x * x, axis=-1, keepdims=True) + RMS_EPS) * g_ref[...]).astype(BF16)

    def proj(idx):
        return _dot(h, w_ref[:, idx * BRANCH_W:(idx + 1) * BRANCH_W])

    cos = cos_ref[...]
    sin = sin_ref[...]
    lane = lax.broadcasted_iota(jnp.int32, (tm, SLAB), 1)
    first_half = (lane & (HALF // 2)) == 0

    def rope_slabs(acc):
        out = []
        for j in range(BRANCH_W // SLAB):
            y = acc[:, j * SLAB:(j + 1) * SLAB]
            partner = jnp.where(first_half, pltpu.roll(y, SLAB - HALF // 2, 1), pltpu.roll(y, HALF // 2, 1))
            out.append(y * cos + partner * sin)
        return out

    qa_ref[...] = (proj(0) * (LOG2E * HALF ** -0.5)).astype(BF16)
    ak = proj(1)
    ka_ref[...] = ak.astype(BF16)
    kaw_ref[...] = ak
    av = proj(2)
    vaw_ref[...] = av
    if transposed_v:
        for p in range(BRANCH_W // SLAB):
            va_ref[p] = av[:, p * SLAB:(p + 1) * SLAB].T.astype(BF16)
    else:
        va_ref[...] = av.astype(BF16)
    sza_ref[...] = _silu(proj(3)).astype(BF16)

    for j, y in enumerate(rope_slabs(proj(4))):
        qb_ref[:, j * SLAB:(j + 1) * SLAB] = (y * (LOG2E * HALF ** -0.5)).astype(BF16)
    for j, y in enumerate(rope_slabs(proj(5))):
        kbf_ref[:, j * SLAB:(j + 1) * SLAB] = y
        kb_ref[:, j * SLAB:(j + 1) * SLAB] = y.astype(BF16)
    bv = proj(6)
    vbf_ref[...] = bv.reshape(vbf_ref.shape)
    if transposed_v:
        for hd in range(BRANCH_W // SLAB):
            for i in range(tm // VT_BLK):
                vb_ref[hd, i, :SLAB, :] = bv[i * VT_BLK:(i + 1) * VT_BLK, hd * SLAB:(hd + 1) * SLAB].T.astype(BF16)
                vb_ref[hd, i, SLAB:, :] = jnp.ones((VT_ROWS - SLAB, VT_BLK), BF16)
    else:
        vb_ref[...] = bv.astype(BF16)
    szb_ref[...] = _silu(proj(7)).astype(BF16)

    qm_ref[...] = (proj(8) * mem_scale).astype(BF16)
    szm_ref[...] = _silu(proj(9)).astype(BF16)
    for j in range(sg_ref.shape[1] // BRANCH_W):
        sg_ref[:, j * BRANCH_W:(j + 1) * BRANCH_W] = _sigmoid(proj(10 + j)).astype(BF16)


def _in_proj(x, norm_g, w_bf16, cos_tab, sin_tab, *, keep, transposed_v):
    b, s, d = x.shape
    d_in = w_bf16.shape[1]
    tm = min(PROJ_ROWS, s)
    nt = s // tm
    first_kept = (s - keep) // tm
    n_gate = d_in - 10 * BRANCH_W

    row = lambda bi, j: (bi, j, 0)
    kept = lambda bi, j: (bi, jnp.maximum(j - first_kept, 0), 0)
    act = lambda dt: jax.ShapeDtypeStruct((b, s, BRANCH_W), dt)
    act_spec = pl.BlockSpec((None, tm, BRANCH_W), row)
    win = jax.ShapeDtypeStruct((b, keep, BRANCH_W), F32)
    win_spec = pl.BlockSpec((None, tm, BRANCH_W), kept)
    if transposed_v:
        va = jax.ShapeDtypeStruct((b, BRANCH_W // SLAB, SLAB, s), BF16)
        va_spec = pl.BlockSpec((None, BRANCH_W // SLAB, SLAB, tm), lambda bi, j: (bi, 0, 0, j))
        vb = jax.ShapeDtypeStruct((b, BRANCH_W // SLAB, s // VT_BLK, VT_ROWS, VT_BLK), BF16)
        vb_spec = pl.BlockSpec((None, BRANCH_W // SLAB, tm // VT_BLK, VT_ROWS, VT_BLK), lambda bi, j: (bi, 0, j, 0, 0))
    else:
        va, va_spec, vb, vb_spec = act(BF16), act_spec, act(BF16), act_spec

    out_shape = [act(BF16), act(BF16), va, win, win,
                 act(BF16), act(BF16), act(F32), vb, jax.ShapeDtypeStruct((b, s, BRANCH_W // SLAB, SLAB), F32),
                 act(BF16), act(BF16), act(BF16), act(BF16),
                 jax.ShapeDtypeStruct((b, s, n_gate), BF16)]
    out_specs = [act_spec, act_spec, va_spec, win_spec, win_spec,
                 act_spec, act_spec, act_spec, vb_spec,
                 pl.BlockSpec((None, tm, BRANCH_W // SLAB, SLAB), lambda bi, j: (bi, j, 0, 0)),
                 act_spec, act_spec, act_spec, act_spec,
                 pl.BlockSpec((None, tm, n_gate), row)]
    return pl.pallas_call(
        functools.partial(_in_proj_kernel, transposed_v=transposed_v, mem_scale=LOG2E * SLAB ** -0.5),
        grid=(b, nt),
        in_specs=[pl.BlockSpec((None, tm, d), row),
                  pl.BlockSpec((1, d), lambda bi, j: (0, 0)),
                  pl.BlockSpec((d, d_in), lambda bi, j: (0, 0), pipeline_mode=pl.Buffered(1)),
                  pl.BlockSpec((tm, SLAB), lambda bi, j: (j, 0)),
                  pl.BlockSpec((tm, SLAB), lambda bi, j: (j, 0))],
        out_specs=out_specs,
        out_shape=out_shape,
        compiler_params=pltpu.CompilerParams(dimension_semantics=("arbitrary", "arbitrary"),
                                             vmem_limit_bytes=VMEM_LIMIT),
        name="in_proj",
    )(x, norm_g, w_bf16, cos_tab, sin_tab)


def _chunk_attn_prompt_kernel(q_ref, k_ref, vt_ref, bias_ref, sz_ref, o_ref):
    s_len = q_ref.shape[0]
    for g in range(s_len // Q_BLK):
        q0 = g * Q_BLK
        q2 = _split_pair(q_ref[q0:q0 + Q_BLK, :])
        start = max(q0 - A_PAST, 0)
        off = start - (q0 - A_PAST)
        s = _dot_nt(k_ref[start:start + BAND, :], q2) + bias_ref[off:off + BAND, :]
        m = jnp.max(s, axis=0, keepdims=True)
        p = jnp.exp2(s - m)
        l = jnp.sum(p, axis=0, keepdims=True)
        pb = p.astype(BF16)
        acc = jnp.zeros((SLAB, 2 * Q_BLK), F32)
        for i in range(BAND // SLAB):
            lo = start + i * SLAB
            acc = acc + _dot(vt_ref[:, lo:lo + SLAB], pb[i * SLAB:(i + 1) * SLAB, :])
        acc = acc * (1.0 / l)
        o_t = jnp.concatenate([acc[:HALF, :Q_BLK], acc[HALF:, Q_BLK:]], axis=0)
        o = o_t.T * sz_ref[q0:q0 + Q_BLK, :].astype(F32)
        o_ref[q0:q0 + Q_BLK, :] = o.astype(BF16)


def _chunk_attn_prompt(qa, ka, va_t, bias_t, sza):
    b, s, _ = qa.shape
    n_pair = BRANCH_W // SLAB
    slab = pl.BlockSpec((None, s, SLAB), lambda bi, p: (bi, 0, p))
    return pl.pallas_call(
        _chunk_attn_prompt_kernel,
        grid=(b, n_pair),
        in_specs=[slab, slab,
                  pl.BlockSpec((None, None, SLAB, s), lambda bi, p: (bi, p, 0, 0)),
                  pl.BlockSpec((None, BAND + A_PAST, 2 * Q_BLK), lambda bi, p: (p, 0, 0)),
                  slab],
        out_specs=slab,
        out_shape=jax.ShapeDtypeStruct((b, s, BRANCH_W), BF16),
        compiler_params=pltpu.CompilerParams(dimension_semantics=("arbitrary", "arbitrary"),
                                             vmem_limit_bytes=VMEM_LIMIT),
        name="chunk_attn_prompt",
    )(qa, ka, va_t, bias_t, sza)


def _diff_attn_schedule(n_q):
    steps = [(qi, qi) for qi in range(n_q)]
    steps += [(qi, kv) for kv in range(n_q) for qi in range(kv + 1, n_q)]
    return steps


def _diff_attn_prompt_kernel(sched_ref, q_ref, k_ref, vt_ref, sz_ref, lq1_ref, lk1_ref, lq2_ref, lk2_ref, subln_ref,
                             o_ref, q2_ref, kz_ref, m_ref, acc_ref, s_ref, p_ref, *, lam_init):
    n_q = q_ref.shape[0] // DQ_BLK
    n_steps = sched_ref.shape[1]
    sub_blocks = KV_STEP // VT_BLK

    @pl.when((pl.program_id(0) == 0) & (pl.program_id(1) == 0))
    def _():
        k_chunk = jnp.right_shift(lax.broadcasted_iota(jnp.int32, kz_ref.shape[1:], 0), CHUNK_SHIFT)
        chunk_id = lax.broadcasted_iota(jnp.int32, kz_ref.shape[1:], 1)
        kz_ref[0] = jnp.zeros(kz_ref.shape[1:], BF16)
        kz_ref[1] = jnp.where(k_chunk == chunk_id, 1.0, 0.0).astype(BF16)
        q_col = lax.broadcasted_iota(jnp.int32, (SLAB, 2 * DQ_BLK), 1)
        q_chunk = jnp.right_shift(q_col & (DQ_BLK - 1), CHUNK_SHIFT)
        hidden = lax.broadcasted_iota(jnp.int32, (SLAB, 2 * DQ_BLK), 0) > q_chunk
        for qi in range(n_q):
            q2_ref[qi, SLAB:, :] = jnp.where(hidden, NEG_INF, 0.0).astype(BF16)

    dim = lax.broadcasted_iota(jnp.int32, (SLAB, DQ_BLK), 0)
    for qi in range(n_q):
        q_t = q_ref[qi * DQ_BLK:(qi + 1) * DQ_BLK, :].astype(F32).T.astype(BF16)
        q2_ref[qi, :SLAB, :DQ_BLK] = jnp.where(dim < HALF, q_t, jnp.zeros_like(q_t))
        q2_ref[qi, :SLAB, DQ_BLK:] = jnp.where(dim >= HALF, q_t, jnp.zeros_like(q_t))
    m_ref[...] = jnp.full(m_ref.shape, NEG_INF, F32)
    acc_ref[...] = jnp.zeros(acc_ref.shape, F32)
    p_ref[PIPE_SLOTS - 1] = jnp.zeros(p_ref.shape[1:], BF16)

    n_groups = 2 * DQ_BLK // COL_GROUP

    def cols(g):
        return slice(g * COL_GROUP, (g + 1) * COL_GROUP)

    def scores(t, slot, g):
        t = jnp.minimum(t, n_steps - 1)
        qi = sched_ref[0, t]
        kv = sched_ref[1, t]
        k0 = pl.multiple_of(kv * KV_STEP, KV_STEP)
        keys = jnp.concatenate([k_ref[pl.ds(k0, KV_STEP), :], kz_ref[jnp.where(qi == kv, 1, 0)]], axis=1)
        s_ref[slot, :, cols(g)] = _dot(keys, q2_ref[qi, :, cols(g)])

    def softmax(t, slot, g):
        qi = sched_ref[0, t]
        s = s_ref[slot, :, cols(g)]
        m_old = m_ref[qi, :, cols(g)]
        m = jnp.maximum(m_old, jnp.max(s, axis=0, keepdims=True))
        m_ref[qi, :, cols(g)] = m
        p_ref[slot, :, cols(g)] = jnp.exp2(s - m).astype(BF16)
        return jnp.exp2(m_old - m)

    def values(t, slot, g, alpha):
        t = jnp.maximum(t, 0)
        qi = sched_ref[0, t]
        kv = sched_ref[1, t]
        pv = _dot(vt_ref[sub_blocks * kv], p_ref[slot, :VT_BLK, cols(g)])
        for i in range(1, sub_blocks):
            pv = pv + _dot(vt_ref[sub_blocks * kv + i], p_ref[slot, i * VT_BLK:(i + 1) * VT_BLK, cols(g)])
        acc_ref[qi, :, cols(g)] = alpha * acc_ref[qi, :, cols(g)] + pv

    def ring_steps(j, alphas):
        for u in range(BODY_STEPS):
            t = BODY_STEPS * j + u
            new_alphas = []
            for g in range(n_groups):
                scores(t + 2, (u + 2) % PIPE_SLOTS, g)
                new_alphas.append(softmax(t, u % PIPE_SLOTS, g))
                values(t - 1, (u - 1) % PIPE_SLOTS, g, alphas[g])
            alphas = tuple(new_alphas)
        return alphas

    for g in range(n_groups):
        scores(0, 0, g)
        scores(1, 1, g)
    alphas = lax.fori_loop(0, n_steps // BODY_STEPS, ring_steps,
                           tuple(jnp.ones((1, COL_GROUP), F32) for _ in range(n_groups)))
    for g in range(n_groups):
        values(n_steps - 1, (n_steps - 1) % PIPE_SLOTS, g, alphas[g])

    lam = _diff_lambda(lq1_ref, lk1_ref, lq2_ref, lk2_ref, lam_init)

    def finish(qi, carry):
        q0 = pl.multiple_of(qi * DQ_BLK, DQ_BLK)
        acc = acc_ref[qi, :SLAB, :] * (1.0 / acc_ref[qi, SLAB:SLAB + 1, :])
        o_t = acc[:, :DQ_BLK] - lam * acc[:, DQ_BLK:]
        o_t = o_t * lax.rsqrt(jnp.mean(o_t * o_t, axis=0, keepdims=True) + RMS_EPS)
        o_t = o_t * subln_ref[...] * (1.0 - lam_init)
        o = o_t.T * sz_ref[pl.ds(q0, DQ_BLK), :].astype(F32)
        o_ref[pl.ds(q0, DQ_BLK), :] = o.astype(BF16)
        return carry

    lax.fori_loop(0, n_q, finish, 0, unroll=True)


def _diff_attn_prompt(qb, kb, vb_t, szb, lq1, lk1, lq2, lk2, subln_col, lam_init):
    b, s, _ = qb.shape
    n_head = BRANCH_W // SLAB
    n_q = s // DQ_BLK
    steps = _diff_attn_schedule(n_q)
    assert len(steps) % BODY_STEPS == 0, "one loop body covers BODY_STEPS steps"
    assert KV_STEP // CHUNK <= SLAB, "chunk ids of a diagonal block must fit the spare contraction lanes"
    sched = jnp.asarray([[qi for qi, _ in steps], [kv for _, kv in steps]], jnp.int32)
    slab = pl.BlockSpec((None, s, SLAB), lambda bi, hd: (bi, 0, hd))
    small = lambda shape: pl.BlockSpec(shape, lambda bi, hd: (0, 0))
    return pl.pallas_call(
        functools.partial(_diff_attn_prompt_kernel, lam_init=lam_init),
        grid=(b, n_head),
        in_specs=[pl.BlockSpec(memory_space=pltpu.SMEM), slab, slab,
                  pl.BlockSpec((None, None, s // VT_BLK, VT_ROWS, VT_BLK), lambda bi, hd: (bi, hd, 0, 0, 0)),
                  slab,
                  small(lq1.shape), small(lk1.shape), small(lq2.shape), small(lk2.shape), small(subln_col.shape)],
        out_specs=slab,
        out_shape=jax.ShapeDtypeStruct((b, s, BRANCH_W), BF16),
        scratch_shapes=[pltpu.VMEM((n_q, 2 * SLAB, 2 * DQ_BLK), BF16),
                        pltpu.VMEM((2, KV_STEP, SLAB), BF16),
                        pltpu.VMEM((n_q, 1, 2 * DQ_BLK), F32),
                        pltpu.VMEM((n_q, VT_ROWS, 2 * DQ_BLK), F32),
                        pltpu.VMEM((PIPE_SLOTS, KV_STEP, 2 * DQ_BLK), F32),
                        pltpu.VMEM((PIPE_SLOTS, KV_STEP, 2 * DQ_BLK), BF16)],
        compiler_params=pltpu.CompilerParams(dimension_semantics=("arbitrary", "arbitrary"),
                                             vmem_limit_bytes=VMEM_LIMIT),
        name="diff_attn_prompt",
    )(sched, qb, kb, vb_t, szb, lq1, lk1, lq2, lk2, subln_col)


def _mem_kv_kernel(mem_ref, g_ref, w_ref, kf_ref, vf_ref, kb_ref, vb_ref):
    x = mem_ref[...]
    h = (x * lax.rsqrt(jnp.mean(x * x, axis=-1, keepdims=True) + RMS_EPS) * g_ref[...]).astype(BF16)
    k = _dot(h, w_ref[:, :BRANCH_W])
    v = _dot(h, w_ref[:, BRANCH_W:])
    kf_ref[...] = k.reshape(kf_ref.shape)
    vf_ref[...] = v.reshape(vf_ref.shape)
    kb_ref[...] = k.astype(BF16)
    vb_ref[...] = v.astype(BF16)


def _mem_kv(mem, norm_g, w_bf16):
    b, n, d = mem.shape
    blk = lambda width: pl.BlockSpec((None, n, width), lambda bi: (bi, 0, 0))
    out = lambda dt: jax.ShapeDtypeStruct((b, n, BRANCH_W), dt)
    heads = jax.ShapeDtypeStruct((b, n, BRANCH_W // SLAB, SLAB), F32)
    heads_blk = pl.BlockSpec((None, n, BRANCH_W // SLAB, SLAB), lambda bi: (bi, 0, 0, 0))
    return pl.pallas_call(
        _mem_kv_kernel,
        grid=(b,),
        in_specs=[blk(d), pl.BlockSpec((1, d), lambda bi: (0, 0)),
                  pl.BlockSpec((d, 2 * BRANCH_W), lambda bi: (0, 0))],
        out_specs=[heads_blk, heads_blk, blk(BRANCH_W), blk(BRANCH_W)],
        out_shape=[heads, heads, out(BF16), out(BF16)],
        compiler_params=pltpu.CompilerParams(dimension_semantics=("arbitrary",)),
        name="mem_kv",
    )(mem, norm_g, w_bf16)


def _mem_attn_kernel(q_ref, k_ref, v_ref, sz_ref, o_ref):
    n_mem = k_ref.shape[0]
    k_all = k_ref[...].reshape(n_mem, BRANCH_W).astype(BF16)
    v_all = v_ref[...].reshape(n_mem, BRANCH_W).astype(BF16)
    for hd in range(BRANCH_W // SLAB):
        cols = slice(hd * SLAB, (hd + 1) * SLAB)
        s = _dot_nt(q_ref[:, cols], k_all[:, cols])
        m = jnp.max(s, axis=-1, keepdims=True)
        p = jnp.exp2(s - m).astype(BF16)
        v = v_all[:, cols]
        ol = _dot(p, jnp.concatenate([v, jnp.ones_like(v)], axis=1))
        o = ol[:, :SLAB] / ol[:, SLAB:]
        o_ref[:, cols] = (o * sz_ref[:, cols].astype(F32)).astype(BF16)


def _mem_attn(qm, mk, mv, szm):
    b, s, _ = qm.shape
    tq = min(MEM_ROWS, s)
    row = pl.BlockSpec((None, tq, BRANCH_W), lambda bi, j: (bi, j, 0))
    mem = pl.BlockSpec((None,) + mk.shape[1:], lambda bi, j: (bi,) + (0,) * (mk.ndim - 1))
    return pl.pallas_call(
        _mem_attn_kernel,
        grid=(b, s // tq),
        in_specs=[row, mem, mem, row],
        out_specs=row,
        out_shape=jax.ShapeDtypeStruct((b, s, BRANCH_W), BF16),
        compiler_params=pltpu.CompilerParams(dimension_semantics=("arbitrary", "arbitrary")),
        name="mem_attn",
    )(qm, mk, mv, szm)


def _merge_kernel(ga_ref, gb_ref, gm_ref, sg_ref, x_ref, wa_ref, wb_ref, wm_ref, wo_ref, nf_ref, y_ref):
    d = x_ref.shape[1]
    mixed = sg_ref[:, :d].astype(F32) * _dot(ga_ref[...], wa_ref[...])
    mixed = mixed + sg_ref[:, d:2 * d].astype(F32) * _dot(gb_ref[...], wb_ref[...])
    mixed = mixed + sg_ref[:, 2 * d:].astype(F32) * _dot(gm_ref[...], wm_ref[...])
    hsum = x_ref[...] + _dot(mixed.astype(BF16), wo_ref[...])
    y_ref[...] = hsum * lax.rsqrt(jnp.mean(hsum * hsum, axis=-1, keepdims=True) + RMS_EPS) * nf_ref[...]


def _merge(ga, gb, gm, sg, x, wa, wb, wm, wo, norm_final):
    b, s, d = x.shape
    tm = min(MERGE_ROWS, s)
    row = lambda width: pl.BlockSpec((None, tm, width), lambda bi, j: (bi, j, 0))
    full = lambda a: pl.BlockSpec(a.shape, lambda bi, j: (0, 0))
    return pl.pallas_call(
        _merge_kernel,
        grid=(b, s // tm),
        in_specs=[row(BRANCH_W), row(BRANCH_W), row(BRANCH_W), row(sg.shape[2]), row(d),
                  full(wa), full(wb), full(wm), full(wo), full(norm_final)],
        out_specs=row(d),
        out_shape=jax.ShapeDtypeStruct((b, s, d), F32),
        compiler_params=pltpu.CompilerParams(dimension_semantics=("arbitrary", "arbitrary"),
                                             vmem_limit_bytes=VMEM_LIMIT),
        name="merge",
    )(ga, gb, gm, sg, x, wa, wb, wm, wo, norm_final)


def _chunk_attn_sample_kernel(q_ref, kt_ref, vt_ref, kn_ref, vn_ref, bias_ref, sz_ref, o_ref):
    t = q_ref.shape[0]
    past = kt_ref.shape[2]
    lane = lax.broadcasted_iota(jnp.int32, (t, SLAB), 1)
    for p in range(BRANCH_W // SLAB):
        cols = slice(p * SLAB, (p + 1) * SLAB)
        q2 = _split_pair(q_ref[:, cols])
        kt = kt_ref[2 * p:2 * p + 2].reshape(SLAB, past).astype(BF16)
        vt = vt_ref[2 * p:2 * p + 2].reshape(SLAB, past).astype(BF16)
        s_c = _dot(q2, kt) + bias_ref[p, :, :past]
        s_n = _dot_nt(q2, kn_ref[:, cols]) + bias_ref[p, :, past:]
        m = jnp.maximum(jnp.max(s_c, axis=-1, keepdims=True), jnp.max(s_n, axis=-1, keepdims=True))
        p_c = jnp.exp2(s_c - m)
        p_n = jnp.exp2(s_n - m)
        l = jnp.sum(p_c, axis=-1, keepdims=True) + jnp.sum(p_n, axis=-1, keepdims=True)
        o = (_dot_nt(p_c.astype(BF16), vt) + _dot(p_n.astype(BF16), vn_ref[:, cols])) * (1.0 / l)
        o = jnp.where(lane < HALF, o[:t], o[t:])
        o_ref[:, cols] = (o * sz_ref[:, cols].astype(F32)).astype(BF16)


def _chunk_attn_sample(qa, cache_kt, cache_vt, k_new, v_new, bias, sza):
    b, t, _ = qa.shape
    n_new = k_new.shape[1]
    row = pl.BlockSpec((None, t, BRANCH_W), lambda bi: (bi, 0, 0))
    new = pl.BlockSpec((None, n_new, BRANCH_W), lambda bi: (bi, 0, 0))
    cache = pl.BlockSpec((None,) + cache_kt.shape[1:], lambda bi: (bi, 0, 0, 0))
    return pl.pallas_call(
        _chunk_attn_sample_kernel,
        grid=(b,),
        in_specs=[row, cache, cache, new, new, pl.BlockSpec(bias.shape, lambda bi: (0, 0, 0)), row],
        out_specs=row,
        out_shape=jax.ShapeDtypeStruct((b, t, BRANCH_W), BF16),
        compiler_params=pltpu.CompilerParams(dimension_semantics=("arbitrary",)),
        name="chunk_attn_sample",
    )(qa, cache_kt, cache_vt, k_new, v_new, bias, sza)


def _diff_attn_sample_kernel(q_ref, kt_ref, vc_ref, kn_ref, vn_ref, nbias_ref, sz_ref,
                             lq1_ref, lk1_ref, lq2_ref, lk2_ref, subln_ref, o_ref, *, lam_init):
    t = q_ref.shape[0]
    past = kt_ref.shape[2]
    lam = _diff_lambda(lq1_ref, lk1_ref, lq2_ref, lk2_ref, lam_init)
    v_cache = vc_ref[...].reshape(past, BRANCH_W).astype(BF16)
    for hd in range(BRANCH_W // SLAB):
        cols = slice(hd * SLAB, (hd + 1) * SLAB)
        q2 = _split_pair(q_ref[:, cols])
        kt = kt_ref[2 * hd:2 * hd + 2].reshape(SLAB, past).astype(BF16)
        s_c = _dot(q2, kt)
        s_n = _dot_nt(q2, kn_ref[:, cols]) + nbias_ref[...]
        m = jnp.maximum(jnp.max(s_c, axis=-1, keepdims=True), jnp.max(s_n, axis=-1, keepdims=True))
        p_c = jnp.exp2(s_c - m)
        p_n = jnp.exp2(s_n - m)
        l = jnp.sum(p_c, axis=-1, keepdims=True) + jnp.sum(p_n, axis=-1, keepdims=True)
        o = _dot(p_c.astype(BF16), v_cache[:, cols]) + _dot(p_n.astype(BF16), vn_ref[:, cols])
        o = o * (1.0 / l)
        od = o[:t] - lam * o[t:]
        od = od * lax.rsqrt(jnp.mean(od * od, axis=-1, keepdims=True) + RMS_EPS) * subln_ref[...] * (1.0 - lam_init)
        o_ref[:, cols] = (od * sz_ref[:, cols].astype(F32)).astype(BF16)


def _diff_attn_sample(qb, cache_kt, cache_v, k_new, v_new, nbias, szb, lq1, lk1, lq2, lk2, subln_row, lam_init):
    b, t, _ = qb.shape
    n_new = k_new.shape[1]
    tok = pl.BlockSpec((None, t, BRANCH_W), lambda bi: (bi, 0, 0))
    new = pl.BlockSpec((None, n_new, BRANCH_W), lambda bi: (bi, 0, 0))
    whole = lambda a: pl.BlockSpec((None,) + a.shape[1:], lambda bi: (bi,) + (0,) * (a.ndim - 1))
    small = lambda a: pl.BlockSpec(a.shape, lambda bi: (0, 0))
    return pl.pallas_call(
        functools.partial(_diff_attn_sample_kernel, lam_init=lam_init),
        grid=(b,),
        in_specs=[tok, whole(cache_kt), whole(cache_v), new, new, small(nbias), tok,
                  small(lq1), small(lk1), small(lq2), small(lk2), small(subln_row)],
        out_specs=tok,
        out_shape=jax.ShapeDtypeStruct((b, t, BRANCH_W), BF16),
        compiler_params=pltpu.CompilerParams(dimension_semantics=("arbitrary",), vmem_limit_bytes=VMEM_LIMIT),
        name="diff_attn_sample",
    )(qb, cache_kt, cache_v, k_new, v_new, nbias, szb, lq1, lk1, lq2, lk2, subln_row)


def _rope_tables(pos):
    inv = 1.0 / (ROPE_THETA ** (jnp.arange(0, HALF, 2, dtype=F32) / HALF))
    ang = pos.astype(F32)[:, None] * inv[None, :]
    cos = jnp.cos(ang)
    sin = jnp.sin(ang)
    return jnp.tile(cos, (1, SLAB // (HALF // 2))), jnp.tile(jnp.concatenate([-sin, sin], axis=1), (1, SLAB // HALF))


def _bias_lookup(rb_ref, pair, dist, head_in_pair):
    idx = jnp.clip(dist, -REL_CLIP, REL_CLIP) + REL_CLIP

    def entry(k, acc):
        v = jnp.where(head_in_pair == 1, rb_ref[2 * pair + 1, k], rb_ref[2 * pair, k])
        return jnp.where(idx == k, v, acc)

    return LOG2E * lax.fori_loop(0, 2 * REL_CLIP + 1, entry, jnp.zeros(dist.shape, F32))


def _prompt_bias_kernel(rb_ref, o_ref):
    pair = pl.program_id(0)

    def rows(u0, n):
        u = u0 + lax.broadcasted_iota(jnp.int32, (n, 2 * Q_BLK), 0)
        c = lax.broadcasted_iota(jnp.int32, (n, 2 * Q_BLK), 1)
        ql = c & (Q_BLK - 1)
        visible = ((ql < CHUNK) & (u < A_PAST + CHUNK)) | ((ql >= CHUNK) & (u >= CHUNK))
        return ql + A_PAST - u, jnp.where(c >= Q_BLK, 1, 0), visible

    dist, head_in_pair, visible = rows(BAND_CONST, BAND - BAND_CONST)
    o_ref[BAND_CONST:BAND, :] = jnp.where(visible, _bias_lookup(rb_ref, pair, dist, head_in_pair), NEG_INF)
    _, head_in_pair, visible = rows(0, BAND_CONST)
    far = LOG2E * jnp.where(head_in_pair == 1, rb_ref[2 * pair + 1, 2 * REL_CLIP], rb_ref[2 * pair, 2 * REL_CLIP])
    o_ref[:BAND_CONST, :] = jnp.where(visible, far, NEG_INF)
    o_ref[BAND:, :] = jnp.full((o_ref.shape[0] - BAND, 2 * Q_BLK), NEG_INF, F32)


def _prompt_bias_table(rel_bias):
    n_pair = rel_bias.shape[0] // 2
    rows = BAND + A_PAST
    return pl.pallas_call(
        _prompt_bias_kernel,
        grid=(n_pair,),
        in_specs=[pl.BlockSpec(memory_space=pltpu.SMEM)],
        out_specs=pl.BlockSpec((None, rows, 2 * Q_BLK), lambda p: (p, 0, 0)),
        out_shape=jax.ShapeDtypeStruct((n_pair, rows, 2 * Q_BLK), F32),
        compiler_params=pltpu.CompilerParams(dimension_semantics=("arbitrary",)),
        name="prompt_bias",
    )(rel_bias)


def _sample_bias_kernel(rb_ref, o_ref, *, p_len, t):
    pair = pl.program_id(0)
    r = lax.broadcasted_iota(jnp.int32, o_ref.shape, 0)
    j = lax.broadcasted_iota(jnp.int32, o_ref.shape, 1)
    head_in_pair = jnp.where(r >= t, 1, 0)
    dist = p_len + r - t * head_in_pair - j
    o_ref[...] = jnp.where(j < p_len + t, _bias_lookup(rb_ref, pair, dist, head_in_pair), NEG_INF)


def _sample_bias_table(rel_bias, p_len, t, n_keys):
    n_pair = rel_bias.shape[0] // 2
    return pl.pallas_call(
        functools.partial(_sample_bias_kernel, p_len=p_len, t=t),
        grid=(n_pair,),
        in_specs=[pl.BlockSpec(memory_space=pltpu.SMEM)],
        out_specs=pl.BlockSpec((None, 2 * t, n_keys), lambda p: (p, 0, 0)),
        out_shape=jax.ShapeDtypeStruct((n_pair, 2 * t, n_keys), F32),
        compiler_params=pltpu.CompilerParams(dimension_semantics=("arbitrary",)),
        name="sample_bias",
    )(rel_bias)


def _round_up(n, m):
    return (n + m - 1) // m * m


def kernel(x_prompt, x_sample, cache_a_k, cache_a_v, cache_b_k, cache_b_v, cache_mem_k, cache_mem_v, mem_prompt, norm_in, w_in, rel_bias, lambda_q1, lambda_k1, lambda_q2, lambda_k2, subln, norm_mem, w_mem_kv, w_branch_a, w_branch_b, w_branch_m, w_out, norm_final):
    depth = w_in.shape[0]
    assert depth == 1, "kernels are written for the single-layer step"
    bsz, seq, d_model = x_prompt.shape
    dec_b, dec_t, _ = x_sample.shape
    past = cache_b_k.shape[2]
    a_cache = cache_a_k.shape[2]
    n_mem = mem_prompt.shape[1]
    keep = min(A_PAST, seq)
    lam_init = 0.8 - 0.6 * math.exp(-0.3 * 0)
    l = 0

    w_in_b = w_in[l].astype(BF16)
    w_mem_b = w_mem_kv[l].astype(BF16)
    wa, wb, wm, wo = (w[l].astype(BF16) for w in (w_branch_a, w_branch_b, w_branch_m, w_out))
    g_in = norm_in[l][None, :]
    g_mem = norm_mem[l][None, :]
    g_final = norm_final[None, :]
    lq1, lk1, lq2, lk2 = (v[l][None, :] for v in (lambda_q1, lambda_k1, lambda_q2, lambda_k2))

    cos_p, sin_p = _rope_tables(jnp.arange(seq))
    (qa, ka, va_t, ka_keep, va_keep, qb, kb, kb_f32, vb_t, vb_f32, qm, sza, szb, szm, sg) = _in_proj(
        x_prompt, g_in, w_in_b, cos_p, sin_p, keep=keep, transposed_v=True)
    ga = _chunk_attn_prompt(qa, ka, va_t, _prompt_bias_table(rel_bias[l]), sza)
    gb = _diff_attn_prompt(qb, kb, vb_t, szb, lq1, lk1, lq2, lk2, subln[l][:, None], lam_init)
    mk_f32, mv_f32, mk, mv = _mem_kv(mem_prompt, g_mem, w_mem_b)
    gm = _mem_attn(qm, mk, mv, szm)
    y_prompt = _merge(ga, gb, gm, sg, x_prompt, wa, wb, wm, wo, g_final)

    n_tok = dec_b * dec_t
    pos_s = past + jnp.tile(jnp.arange(dec_t), dec_b)
    cos_s, sin_s = _rope_tables(pos_s)
    xs = x_sample.reshape(1, n_tok, d_model)
    (qa_s, ka_s, va_s, ka_s32, va_s32, qb_s, kb_s, kb_s32, vb_s, vb_s32, qm_s, sza_s, szb_s, szm_s, sg_s) = _in_proj(
        xs, g_in, w_in_b, cos_s, sin_s, keep=n_tok, transposed_v=False)
    per_req = lambda a: a.reshape(dec_b, dec_t, a.shape[-1])

    n_new = _round_up(dec_t, SLAB)
    pad_new = lambda a: jnp.pad(per_req(a), ((0, 0), (0, n_new - dec_t), (0, 0)))
    dims_major = lambda cache: jnp.transpose(cache, (0, 2, 3, 1))
    ga_s = _chunk_attn_sample(per_req(qa_s), dims_major(cache_a_k[l]), dims_major(cache_a_v[l]),
                              pad_new(ka_s), pad_new(va_s),
                              _sample_bias_table(rel_bias[l], a_cache, dec_t, a_cache + n_new), per_req(sza_s))

    nbias = jnp.where(jnp.arange(n_new) < dec_t, 0.0, NEG_INF).astype(F32)[None, :]
    gb_s = _diff_attn_sample(per_req(qb_s), dims_major(cache_b_k[l]), cache_b_v[l],
                             pad_new(kb_s), pad_new(vb_s), nbias, per_req(szb_s),
                             lq1, lk1, lq2, lk2, subln[l][None, :], lam_init)
    gm_s = _mem_attn(per_req(qm_s), cache_mem_k[l], cache_mem_v[l], per_req(szm_s))
    flat = lambda a: a.reshape(1, n_tok, a.shape[-1])
    y_sample = _merge(flat(ga_s), flat(gb_s), flat(gm_s), sg_s, xs, wa, wb, wm, wo, g_final).reshape(x_sample.shape)

    a_heads, a_dim = cache_a_k.shape[3:]
    bk_heads, bk_dim = cache_b_k.shape[3:]
    bv_heads, bv_dim = cache_b_v.shape[3:]
    m_heads, m_dim = cache_mem_k.shape[3:]
    return (y_prompt, y_sample,
            ka_keep.reshape(1, bsz, keep, a_heads, a_dim), va_keep.reshape(1, bsz, keep, a_heads, a_dim),
            kb_f32.reshape(1, bsz, seq, bk_heads, bk_dim), vb_f32.reshape(1, bsz, seq, bv_heads, bv_dim),
            mk_f32.reshape(1, bsz, n_mem, m_heads, m_dim), mv_f32.reshape(1, bsz, n_mem, m_heads, m_dim),
            ka_s32.reshape(1, dec_b, dec_t, a_heads, a_dim), va_s32.reshape(1, dec_b, dec_t, a_heads, a_dim),
            kb_s32.reshape(1, dec_b, dec_t, bk_heads, bk_dim), vb_s32.reshape(1, dec_b, dec_t, bv_heads, bv_dim))
```
